```python
import math
import jax
import jax.numpy as jnp
from jax import lax
import numpy as np

D_MODEL = 2048
BATCH = 4
SEQ = 4096
DEPTH = 4

GRID_W = 64
CTX_LEN = 256
N_MIXERS = 4
ALPHA = (2.0 * DEPTH) ** 0.25
BETA = (8.0 * DEPTH) ** -0.25
EPS = 1e-5

SSD_D_INNER = 2 * D_MODEL
SSD_HEAD_DIM = 64
SSD_HEADS = SSD_D_INNER // SSD_HEAD_DIM
SSD_GROUPS = 8
SSD_HPG = SSD_HEADS // SSD_GROUPS
SSD_STATE = 128
SSD_CONV_W = 3
SSD_CONV_CH = SSD_D_INNER + 2 * SSD_GROUPS * SSD_STATE
SSD_IN = SSD_D_INNER + SSD_CONV_CH + 2 * SSD_HEADS
SSD_CHUNK = 128

RET_HEADS = 8
RET_QK = D_MODEL
RET_V = 2 * D_MODEL
RET_HEAD_K = RET_QK // RET_HEADS
RET_HEAD_V = RET_V // RET_HEADS
RET_IN = 2 * RET_QK + 2 * RET_V
RET_CHUNK = 128

GLA_HEADS = 4
GLA_K = D_MODEL // 2
GLA_V = D_MODEL
GLA_HEAD_K = GLA_K // GLA_HEADS
GLA_HEAD_V = GLA_V // GLA_HEADS
GLA_RANK = 16
GLA_TAU = 16.0
GLA_IN = 2 * GLA_K + 2 * GLA_V + 2 * GLA_RANK
GLA_CHUNK = 64

HGRN_HEAD_K = 128
HGRN_HEADS = D_MODEL // HGRN_HEAD_K
HGRN_HEAD_V = D_MODEL // HGRN_HEADS
HGRN_K = HGRN_HEADS * HGRN_HEAD_K
HGRN_V = HGRN_HEADS * HGRN_HEAD_V
HGRN_IN = HGRN_K + 2 * HGRN_V + 2 * HGRN_K
HGRN_CHUNK = 64

FFN_HIDDEN = 5504
FFN_CONV_W = 3

kernel_name = 'hybrid_interleaved_ssd_retnet_gla_hgrn2_dit'


def n_occ(kind):
    return (DEPTH - kind + N_MIXERS - 1) // N_MIXERS


def layer_norm(x, g, b):
    xf = x.astype(jnp.float32)
    mu = jnp.mean(xf, axis=-1, keepdims=True)
    var = jnp.mean(jnp.square(xf - mu), axis=-1, keepdims=True)
    return ((xf - mu) * lax.rsqrt(var + EPS)).astype(x.dtype) * g + b


def rms_norm(x, g):
    xf = x.astype(jnp.float32)
    return (xf * lax.rsqrt(jnp.mean(jnp.square(xf), axis=-1, keepdims=True) + EPS)).astype(x.dtype) * g


def dwconv1d(x, w, b):
    width, ch = w.shape
    pad = width // 2
    y = lax.conv_general_dilated(x, w[:, None, :], (1,), ((pad, pad),),
                                 dimension_numbers=('NWC', 'WIO', 'NWC'), feature_group_count=ch)
    return y + b


def dwconv2d_grid(x, w, b):
    bsz, seqlen, ch = x.shape
    rows = seqlen // GRID_W
    xg = x.reshape(bsz, rows, GRID_W, ch)
    y = lax.conv_general_dilated(xg, w[:, :, None, :], (1, 1), ((1, 1), (1, 1)),
                                 dimension_numbers=('NHWC', 'HWIO', 'NHWC'), feature_group_count=ch)
    return y.reshape(bsz, seqlen, ch) + b


def rotary(x, pos):
    half = x.shape[-1] // 2
    inv_freq = 1.0 / (10000.0 ** jnp.linspace(0.0, 1.0, half, dtype=jnp.float32))
    ang = pos[:, None] * inv_freq[None, :]
    cos = jnp.cos(ang)[None, :, None, :]
    sin = jnp.sin(ang)[None, :, None, :]
    xp = x.reshape(x.shape[:-1] + (half, 2))
    x0, x1 = xp[..., 0], xp[..., 1]
    out = jnp.stack([x0 * cos - x1 * sin, x0 * sin + x1 * cos], axis=-1)
    return out.reshape(x.shape).astype(x.dtype)


def chunked_scalar_decay_scan(q, k, v, log_a, s0, chunk):
    bsz, seqlen, ng, dk = q.shape
    nr, dv = v.shape[-2:]
    nc = seqlen // chunk
    qc = q.reshape(bsz, nc, chunk, ng, dk)
    kc = k.reshape(bsz, nc, chunk, ng, dk)
    vc = v.reshape(bsz, nc, chunk, ng, nr, dv)
    cum = jnp.cumsum(log_a.astype(jnp.float32).reshape(bsz, nc, chunk, ng, nr), axis=2)
    cum = jnp.moveaxis(cum, 2, -1)
    causal = jnp.tril(jnp.ones((chunk, chunk), dtype=bool))
    seg = cum[..., :, None] - cum[..., None, :]
    decay = jnp.exp(jnp.where(causal, seg, -jnp.inf))
    scores = jnp.einsum('bcigk,bcjgk->bcgij', qc, kc)
    y_intra = jnp.einsum('bcgrij,bcjgrv->bcigrv', scores[:, :, :, None] * decay, vc)
    last = cum[..., -1]
    states = jnp.einsum('bcjgk,bcgrj,bcjgrv->bcgrkv', kc, jnp.exp(last[..., None] - cum), vc)
    if s0 is None:
        s0 = jnp.zeros((bsz, ng, nr, dk, dv), jnp.float32)

    def step(s, inp):
        a_c, st_c = inp
        return jnp.exp(a_c)[..., None, None] * s + st_c, s

    s_final, s_prev = lax.scan(step, s0, (jnp.moveaxis(last, 1, 0), jnp.moveaxis(states, 1, 0)))
    s_prev = jnp.moveaxis(s_prev, 0, 1)
    y_inter = jnp.einsum('bcigk,bcgrkv,bcgri->bcigrv', qc, s_prev, jnp.exp(cum))
    y = (y_intra + y_inter).reshape(bsz, seqlen, ng, nr, dv)
    return y.astype(v.dtype), s_final


def chunked_vector_decay_scan(q, k, v, log_a, s0, chunk):
    bsz, seqlen, nh, dk = q.shape
    dv = v.shape[-1]
    nc = seqlen // chunk
    qc = q.reshape(bsz, nc, chunk, nh, dk)
    kc = k.reshape(bsz, nc, chunk, nh, dk)
    vc = v.reshape(bsz, nc, chunk, nh, dv)
    cum = jnp.cumsum(log_a.astype(jnp.float32).reshape(bsz, nc, chunk, nh, dk), axis=2)
    ref = cum[:, :, chunk // 2:chunk // 2 + 1]
    q_rel = qc * jnp.exp(cum - ref)
    k_rel = kc * jnp.exp(ref - cum)
    causal = jnp.tril(jnp.ones((chunk, chunk), dtype=bool))
    scores = jnp.where(causal, jnp.einsum('bcihk,bcjhk->bchij', q_rel, k_rel), 0.0)
    y_intra = jnp.einsum('bchij,bcjhv->bcihv', scores, vc)
    last = cum[:, :, -1]
    states = jnp.einsum('bcjhk,bcjhv->bchkv', kc * jnp.exp(last[:, :, None] - cum), vc)
    if s0 is None:
        s0 = jnp.zeros((bsz, nh, dk, dv), jnp.float32)

    def step(s, inp):
        a_c, st_c = inp
        return jnp.exp(a_c)[..., None] * s + st_c, s

    s_final, s_prev = lax.scan(step, s0, (jnp.moveaxis(last, 1, 0), jnp.moveaxis(states, 1, 0)))
    s_prev = jnp.moveaxis(s_prev, 0, 1)
    y_inter = jnp.einsum('bcihk,bchkv->bcihv', qc * jnp.exp(cum), s_prev)
    y = (y_intra + y_inter).reshape(bsz, seqlen, nh, dv)
    return y.astype(v.dtype), s_final


def bidir_scan(scan_fn, chunk, q, fwd, bwd, s0_f, s0_b):
    y_f, s_f = scan_fn(q, *fwd, s0_f, chunk)
    rev = lambda t: jnp.flip(t, axis=1)
    y_b, s_b = scan_fn(rev(q), rev(fwd[0]) if bwd is None else rev(bwd[0]), rev(bwd[1]), rev(bwd[2]), s0_b, chunk)
    return y_f + rev(y_b), s_f, s_b


def context_then_latent(scan_fn, chunk, feats_ctx, feats_lat):
    y_ctx, s_f, s_b = bidir_scan(scan_fn, chunk, *feats_ctx, None, None)
    y_lat, _, _ = bidir_scan(scan_fn, chunk, *feats_lat, s_f, s_b)
    return y_ctx, y_lat


def ssd_mixer(u_ctx, u_lat, w_in, conv_w, conv_b, dt_bias, a_log, d_skip, norm_g, w_out):
    a = -jnp.exp(a_log.astype(jnp.float32)).reshape(2, SSD_GROUPS, SSD_HPG)

    def features(h):
        bsz, seqlen, _ = h.shape
        z, xbc, dt_raw = jnp.split(h @ w_in, [SSD_D_INNER, SSD_D_INNER + SSD_CONV_CH], axis=-1)
        xbc = jax.nn.silu(dwconv1d(xbc, conv_w, conv_b))
        xs, bm, cm = jnp.split(xbc, [SSD_D_INNER, SSD_D_INNER + SSD_GROUPS * SSD_STATE], axis=-1)
        xs = xs.reshape(bsz, seqlen, SSD_GROUPS, SSD_HPG, SSD_HEAD_DIM)
        bm = bm.reshape(bsz, seqlen, SSD_GROUPS, SSD_STATE)
        cm = cm.reshape(bsz, seqlen, SSD_GROUPS, SSD_STATE)
        dt = jax.nn.softplus(dt_raw.astype(jnp.float32).reshape(bsz, seqlen, 2, SSD_GROUPS, SSD_HPG)
                             + dt_bias.reshape(2, SSD_GROUPS, SSD_HPG))
        log_a = dt * a
        fwd = (bm, xs * dt[:, :, 0, :, :, None], log_a[:, :, 0])
        bwd = (bm, xs * dt[:, :, 1, :, :, None], log_a[:, :, 1])
        return (cm, fwd, bwd), (z, xs)

    def output(y, z, xs):
        bsz, seqlen = z.shape[:2]
        y = y + d_skip.reshape(SSD_GROUPS, SSD_HPG, 1) * xs
        y = y.reshape(bsz, seqlen, SSD_D_INNER) * jax.nn.silu(z)
        return rms_norm(y, norm_g) @ w_out

    feats_c, aux_c = features(u_ctx)
    feats_l, aux_l = features(u_lat)
    y_c, y_l = context_then_latent(chunked_scalar_decay_scan, SSD_CHUNK, feats_c, feats_l)
    return output(y_c, *aux_c), output(y_l, *aux_l)


def retention_mixer(u_ctx, u_lat, w_in, decay_logit, gn_g, gn_b, w_out):
    log_gamma = jax.nn.log_sigmoid(decay_logit.astype(jnp.float32))
    ctx_len = u_ctx.shape[1]
    pos_ctx = jnp.arange(ctx_len, dtype=jnp.float32)
    pos_lat = ctx_len + jnp.arange(u_lat.shape[1], dtype=jnp.float32)

    def features(h, pos):
        bsz, seqlen, _ = h.shape
        q, k, v, g = jnp.split(h @ w_in, [RET_QK, 2 * RET_QK, 2 * RET_QK + RET_V], axis=-1)
        q = rotary(q.reshape(bsz, seqlen, RET_HEADS, RET_HEAD_K), pos)
        k = rotary(k.reshape(bsz, seqlen, RET_HEADS, RET_HEAD_K), pos) * RET_HEAD_K ** -0.5
        v = v.reshape(bsz, seqlen, RET_HEADS, 1, RET_HEAD_V)
        la_f = jnp.broadcast_to(log_gamma[0][:, None], (bsz, seqlen, RET_HEADS, 1))
        la_b = jnp.broadcast_to(log_gamma[1][:, None], (bsz, seqlen, RET_HEADS, 1))
        return (q, (k, v, la_f), (k, v, la_b)), (g,)

    def output(y, g):
        bsz, seqlen = g.shape[:2]
        yf = y.reshape(bsz, seqlen, RET_HEADS, RET_HEAD_V).astype(jnp.float32)
        mu = jnp.mean(yf, axis=-1, keepdims=True)
        var = jnp.mean(jnp.square(yf - mu), axis=-1, keepdims=True)
        yn = ((yf - mu) * lax.rsqrt(var + EPS)).reshape(bsz, seqlen, RET_V).astype(y.dtype) * gn_g + gn_b
        return (jax.nn.silu(g) * yn) @ w_out

    feats_c, aux_c = features(u_ctx, pos_ctx)
    feats_l, aux_l = features(u_lat, pos_lat)
    y_c, y_l = context_then_latent(chunked_scalar_decay_scan, RET_CHUNK, feats_c, feats_l)
    return output(y_c, *aux_c), output(y_l, *aux_l)


def gla_mixer(u_ctx, u_lat, w_in, w_alpha_up, alpha_b, norm_g, w_out):
    def features(h):
        bsz, seqlen, _ = h.shape
        q, k, v, r, a_low = jnp.split(
            h @ w_in, [GLA_K, 2 * GLA_K, 2 * GLA_K + GLA_V, 2 * GLA_K + 2 * GLA_V], axis=-1)
        q = q.reshape(bsz, seqlen, GLA_HEADS, GLA_HEAD_K) * GLA_HEAD_K ** -0.5
        k = k.reshape(bsz, seqlen, GLA_HEADS, GLA_HEAD_K)
        v = v.reshape(bsz, seqlen, GLA_HEADS, GLA_HEAD_V)
        gate_logit = jnp.einsum('bldr,drk->bldk', a_low.reshape(bsz, seqlen, 2, GLA_RANK), w_alpha_up) + alpha_b
        log_a = (jax.nn.log_sigmoid(gate_logit.astype(jnp.float32)) / GLA_TAU).reshape(
            bsz, seqlen, 2, GLA_HEADS, GLA_HEAD_K)
        return (q, (k, v, log_a[:, :, 0]), (k, v, log_a[:, :, 1])), (r,)

    def output(y, r):
        bsz, seqlen = r.shape[:2]
        yn = rms_norm(y, norm_g).reshape(bsz, seqlen, GLA_V)
        return (jax.nn.silu(r) * yn) @ w_out

    feats_c, aux_c = features(u_ctx)
    feats_l, aux_l = features(u_lat)
    y_c, y_l = context_then_latent(chunked_vector_decay_scan, GLA_CHUNK, feats_c, feats_l)
    return output(y_c, *aux_c), output(y_l, *aux_l)


def hgrn2_mixer(u_ctx, u_lat, w_in, lower_bound, norm_g, w_out):
    lb = lower_bound.astype(jnp.float32)

    def features(h):
        bsz, seqlen, _ = h.shape
        q, i, g, f = jnp.split(h @ w_in, [HGRN_K, HGRN_K + HGRN_V, HGRN_K + 2 * HGRN_V], axis=-1)
        q = jax.nn.silu(q).reshape(bsz, seqlen, HGRN_HEADS, HGRN_HEAD_K)
        i = i.reshape(bsz, seqlen, HGRN_HEADS, HGRN_HEAD_V)
        f = lb + (1.0 - lb) * jax.nn.sigmoid(f.astype(jnp.float32).reshape(bsz, seqlen, 2, HGRN_K))
        k = (1.0 - f).reshape(bsz, seqlen, 2, HGRN_HEADS, HGRN_HEAD_K)
        log_a = jnp.log(f).reshape(bsz, seqlen, 2, HGRN_HEADS, HGRN_HEAD_K)
        return (q, (k[:, :, 0], i, log_a[:, :, 0]), (k[:, :, 1], i, log_a[:, :, 1])), (g,)

    def output(y, g):
        bsz, seqlen = g.shape[:2]
        yn = rms_norm(y, norm_g).reshape(bsz, seqlen, HGRN_V)
        return (jax.nn.silu(g) * yn) @ w_out

    feats_c, aux_c = features(u_ctx)
    feats_l, aux_l = features(u_lat)
    y_c, y_l = context_then_latent(chunked_vector_decay_scan, HGRN_CHUNK, feats_c, feats_l)
    return output(y_c, *aux_c), output(y_l, *aux_l)


def conv_ffn(u, w_up, conv_w, conv_b, w_down, on_grid):
    gate, up = jnp.split(u @ w_up, 2, axis=-1)
    if on_grid:
        gate = dwconv2d_grid(gate, conv_w, conv_b)
    else:
        gate = dwconv1d(gate, conv_w[FFN_CONV_W // 2], conv_b)
    return (jax.nn.silu(gate) * up) @ w_down


def setup_inputs(seed: int = 0) -> dict:
    key = jax.random.key(seed)
    ks = iter(jax.random.split(key, 64))

    def normal(shape, scale):
        return jax.random.normal(next(ks), shape, jnp.float32) * scale

    def gain(shape):
        return 1.0 + normal(shape, 0.02)

    def uniform(shape, lo, hi):
        return jax.random.uniform(next(ks), shape, jnp.float32, lo, hi)

    n0, n1, n2, n3 = n_occ(0), n_occ(1), n_occ(2), n_occ(3)
    dt0 = jnp.exp(uniform((n0, 2, SSD_HEADS), math.log(1e-3), math.log(1e-1)))
    ret_base = jnp.log(2.0 ** (5.0 + jnp.arange(RET_HEADS, dtype=jnp.float32)) - 1.0)
    return {
        'x': normal((BATCH, SEQ, D_MODEL), 1.0),
        'c': normal((BATCH, D_MODEL), 1.0),
        'ctx': normal((BATCH, CTX_LEN, D_MODEL), 1.0),
        'c_ctx': normal((D_MODEL,), 1.0),
        'mod_w': normal((DEPTH, D_MODEL, 6 * D_MODEL), 0.5 * D_MODEL ** -0.5),
        'mod_b': normal((DEPTH, 6 * D_MODEL), 0.02),
        'ln_mix_g': gain((DEPTH, D_MODEL)),
        'ln_mix_b': normal((DEPTH, D_MODEL), 0.02),
        'ln_ffn_g': gain((DEPTH, D_MODEL)),
        'ln_ffn_b': normal((DEPTH, D_MODEL), 0.02),
        'ffn_w_up': normal((DEPTH, D_MODEL, 2 * FFN_HIDDEN), D_MODEL ** -0.5),
        'ffn_conv_w': normal((DEPTH, FFN_CONV_W, FFN_CONV_W, FFN_HIDDEN), 1.0 / FFN_CONV_W),
        'ffn_conv_b': normal((DEPTH, FFN_HIDDEN), 0.02),
        'ffn_w_down': normal((DEPTH, FFN_HIDDEN, D_MODEL), BETA * FFN_HIDDEN ** -0.5),
        'hgrn_lb_logits': normal((DEPTH, HGRN_K), 0.1),
        'ssd_w_in': normal((n0, D_MODEL, SSD_IN), D_MODEL ** -0.5),
        'ssd_conv_w': normal((n0, SSD_CONV_W, SSD_CONV_CH), SSD_CONV_W ** -0.5),
        'ssd_conv_b': normal((n0, SSD_CONV_CH), 0.02),
        'ssd_dt_bias': dt0 + jnp.log(-jnp.expm1(-dt0)),
        'ssd_a_log': jnp.log(uniform((n0, 2, SSD_HEADS), 1.0, 16.0)),
        'ssd_d': 1.0 + normal((n0, SSD_HEADS), 0.1),
        'ssd_norm_g': gain((n0, SSD_D_INNER)),
        'ssd_w_out': normal((n0, SSD_D_INNER, D_MODEL), BETA * SSD_D_INNER ** -0.5),
        'ret_w_in': normal((n1, D_MODEL, RET_IN), D_MODEL ** -0.5),
        'ret_decay_logit': ret_base + normal((n1, 2, RET_HEADS), 0.05),
        'ret_gn_g': gain((n1, RET_V)),
        'ret_gn_b': normal((n1, RET_V), 0.02),
        'ret_w_out': normal((n1, RET_V, D_MODEL), BETA * RET_V ** -0.5),
        'gla_w_in': normal((n2, D_MODEL, GLA_IN), D_MODEL ** -0.5),
        'gla_w_alpha_up': normal((n2, 2, GLA_RANK, GLA_K), GLA_RANK ** -0.5),
        'gla_alpha_b': normal((n2, 2, GLA_K), 0.1),
        'gla_norm_g': gain((n2, GLA_HEAD_V)),
        'gla_w_out': normal((n2, GLA_V, D_MODEL), BETA * GLA_V ** -0.5),
        'hgrn_w_in': normal((n3, D_MODEL, HGRN_IN), D_MODEL ** -0.5),
        'hgrn_norm_g': gain((n3, HGRN_HEAD_V)),
        'hgrn_w_out': normal((n3, HGRN_V, D_MODEL), BETA * HGRN_V ** -0.5),
    }


def reference(x, c, ctx, c_ctx, mod_w, mod_b, ln_mix_g, ln_mix_b, ln_ffn_g, ln_ffn_b,
              ffn_w_up, ffn_conv_w, ffn_conv_b, ffn_w_down, hgrn_lb_logits,
              ssd_w_in, ssd_conv_w, ssd_conv_b, ssd_dt_bias, ssd_a_log, ssd_d, ssd_norm_g, ssd_w_out,
              ret_w_in, ret_decay_logit, ret_gn_g, ret_gn_b, ret_w_out,
              gla_w_in, gla_w_alpha_up, gla_alpha_b, gla_norm_g, gla_w_out,
              hgrn_w_in, hgrn_norm_g, hgrn_w_out):
    p_lb = jax.nn.softmax(hgrn_lb_logits.astype(jnp.float32), axis=0)
    lower_bounds = jnp.cumsum(p_lb, axis=0) - p_lb[0]
    c_lat = jax.nn.silu(c)
    c_con = jax.nn.silu(c_ctx)
    h_lat, h_ctx = x, ctx
    for i in range(DEPTH):
        kind, j = i % N_MIXERS, i // N_MIXERS
        m_lat = (c_lat @ mod_w[i] + mod_b[i])[:, None, :]
        m_ctx = c_con @ mod_w[i] + mod_b[i]
        sh_m, sc_m, g_m, sh_f, sc_f, g_f = jnp.split(m_lat, 6, axis=-1)
        csh_m, csc_m, cg_m, csh_f, csc_f, cg_f = jnp.split(m_ctx, 6, axis=-1)
        u_lat = h_lat * (1.0 + sc_m) + sh_m
        u_ctx = h_ctx * (1.0 + csc_m) + csh_m
        if kind == 0:
            o_ctx, o_lat = ssd_mixer(u_ctx, u_lat, ssd_w_in[j], ssd_conv_w[j], ssd_conv_b[j], ssd_dt_bias[j],
                                     ssd_a_log[j], ssd_d[j], ssd_norm_g[j], ssd_w_out[j])
        elif kind == 1:
            o_ctx, o_lat = retention_mixer(u_ctx, u_lat, ret_w_in[j], ret_decay_logit[j], ret_gn_g[j],
                                           ret_gn_b[j], ret_w_out[j])
        elif kind == 2:
            o_ctx, o_lat = gla_mixer(u_ctx, u_lat, gla_w_in[j], gla_w_alpha_up[j], gla_alpha_b[j],
                                     gla_norm_g[j], gla_w_out[j])
        else:
            o_ctx, o_lat = hgrn2_mixer(u_ctx, u_lat, hgrn_w_in[j], lower_bounds[i], hgrn_norm_g[j], hgrn_w_out[j])
        h_lat = layer_norm(ALPHA * h_lat + g_m * o_lat, ln_mix_g[i], ln_mix_b[i])
        f_lat = conv_ffn(h_lat * (1.0 + sc_f) + sh_f, ffn_w_up[i], ffn_conv_w[i], ffn_conv_b[i], ffn_w_down[i], True)
        h_lat = layer_norm(ALPHA * h_lat + g_f * f_lat, ln_ffn_g[i], ln_ffn_b[i])
        if i < DEPTH - 1:
            h_ctx = layer_norm(ALPHA * h_ctx + cg_m * o_ctx, ln_mix_g[i], ln_mix_b[i])
            f_ctx = conv_ffn(h_ctx * (1.0 + csc_f) + csh_f, ffn_w_up[i], ffn_conv_w[i], ffn_conv_b[i],
                             ffn_w_down[i], False)
            h_ctx = layer_norm(ALPHA * h_ctx + cg_f * f_ctx, ln_ffn_g[i], ln_ffn_b[i])
    return h_lat
```

```python
import functools
import math

import jax
import jax.numpy as jnp
from jax import lax
from jax.experimental import pallas as pl
from jax.experimental.pallas import tpu as pltpu

F32 = jnp.float32
BF16 = jnp.bfloat16

EPS = 1e-5
GRID_W = 64
SSD_HEAD_DIM = 64
SSD_GROUPS = 8
SSD_STATE = 128
RET_HEADS = 8
GLA_HEADS = 4
GLA_RANK = 16
GLA_TAU = 16.0
HGRN_HEAD_K = 128
LANES = 128
VMEM_LIMIT = 56 * 1024 * 1024
NEG_BIG = -1e30


def _cparams(*sem):
    return pltpu.CompilerParams(dimension_semantics=sem, vmem_limit_bytes=VMEM_LIMIT)


def _sigmoid(x):
    return 1.0 / (1.0 + jnp.exp(-x))


def _silu(x):
    return x * _sigmoid(x)


def _softplus(x):
    return jnp.maximum(x, 0.0) + jnp.log(1.0 + jnp.exp(-jnp.abs(x)))


def _log_sigmoid(x):
    return jnp.minimum(x, 0.0) - jnp.log(1.0 + jnp.exp(-jnp.abs(x)))


def _pow2_tile(limit, *sizes):
    t = 1
    while t * 2 <= limit and all(s % (t * 2) == 0 for s in sizes):
        t *= 2
    return t


def _dot(a, b):
    return jnp.dot(a, b, preferred_element_type=F32)


def _dot_nt(a, b):
    return lax.dot_general(a, b, (((1,), (1,)), ((), ())), preferred_element_type=F32)


def _dot_tn(a, b):
    return lax.dot_general(a, b, (((0,), (0,)), ((), ())), preferred_element_type=F32)


class _Layout:
    def __init__(self, batch, seq, ctx):
        self.batch, self.seq, self.ctx = batch, seq, ctx
        self.m_lat = batch * seq
        self.m_ctx = batch * ctx
        self.m = self.m_lat + self.m_ctx

    def mod_row(self, tile, tm):
        start = tile * tm
        return jnp.where(start < self.m_lat, start // self.seq, self.batch)


def _mod_kernel(c_ref, w_ref, b_ref, o_ref):
    c = _silu(c_ref[...]).astype(BF16)
    o_ref[...] = _dot(c, w_ref[...].astype(BF16)) + b_ref[...]


def _modulation(c_rows, mod_w, mod_b):
    depth, d, n = mod_w.shape
    rows = c_rows.shape[0]
    tn = _pow2_tile(1024, n)
    return pl.pallas_call(
        _mod_kernel,
        grid=(depth, n // tn),
        in_specs=[pl.BlockSpec((rows, d), lambda l, j: (0, 0)),
                  pl.BlockSpec((None, d, tn), lambda l, j: (l, 0, j)),
                  pl.BlockSpec((None, 1, tn), lambda l, j: (l, 0, j))],
        out_specs=pl.BlockSpec((None, rows, tn), lambda l, j: (l, 0, j)),
        out_shape=jax.ShapeDtypeStruct((depth, rows, n), F32),
        compiler_params=_cparams("parallel", "parallel"),
        name="modulation",
    )(c_rows, mod_w, mod_b.reshape(depth, 1, n))


def _modulate_kernel(h_ref, sc_ref, sh_ref, u_ref):
    u_ref[...] = (h_ref[...] * (1.0 + sc_ref[...]) + sh_ref[...]).astype(BF16)


def _modulate(lay, h, mods, layer, sc_blk, sh_blk):
    m, d = h.shape
    tm = _pow2_tile(512, lay.seq, lay.m_ctx)
    mod_spec = lambda blk: pl.BlockSpec((None, None, 1, d), lambda i: (layer, lay.mod_row(i, tm), 0, blk))
    return pl.pallas_call(
        _modulate_kernel,
        grid=(m // tm,),
        in_specs=[pl.BlockSpec((tm, d), lambda i: (i, 0)), mod_spec(sc_blk), mod_spec(sh_blk)],
        out_specs=pl.BlockSpec((tm, d), lambda i: (i, 0)),
        out_shape=jax.ShapeDtypeStruct((m, d), BF16),
        compiler_params=_cparams("parallel"),
        name="modulate",
    )(h, mods, mods)


def _mm_kernel(x_ref, w_ref, o_ref):
    o_ref[...] = _dot(x_ref[...], w_ref[...]).astype(o_ref.dtype)


def _matmul(x, w, tm, tn):
    m, k = x.shape
    n = w.shape[1]
    return pl.pallas_call(
        _mm_kernel,
        grid=(n // tn, m // tm),
        in_specs=[pl.BlockSpec((tm, k), lambda j, i: (i, 0)),
                  pl.BlockSpec((k, tn), lambda j, i: (0, j))],
        out_specs=pl.BlockSpec((tm, tn), lambda j, i: (i, j)),
        out_shape=jax.ShapeDtypeStruct((m, n), BF16),
        compiler_params=_cparams("parallel", "parallel"),
        name="in_proj",
    )(x, w)


def _mm_ln_kernel(*refs, nk, kdim, alpha, rms, emit_u):
    it = iter(refs)
    x_ref, w_ref, h_ref, gate_ref, lng_ref, lnb_ref = (next(it) for _ in range(6))
    sc_ref = sh_ref = rg_ref = u_ref = ssq_ref = None
    if emit_u:
        sc_ref, sh_ref = next(it), next(it)
    if rms:
        rg_ref = next(it)
    hout_ref = next(it)
    if emit_u:
        u_ref = next(it)
    acc_ref = next(it)
    if rms:
        ssq_ref = next(it)
    k = pl.program_id(1)

    @pl.when(k == 0)
    def _():
        acc_ref[...] = jnp.zeros_like(acc_ref)
        if rms:
            ssq_ref[...] = jnp.zeros_like(ssq_ref)

    x = x_ref[...]
    if rms:
        xf = x.astype(F32)
        ssq_ref[...] += jnp.sum(xf * xf, axis=1, keepdims=True)
        x = (xf * rg_ref[...]).astype(BF16)
    acc_ref[...] += _dot(x, w_ref[...])

    @pl.when(k == nk - 1)
    def _():
        o = acc_ref[...]
        if rms:
            o = o * lax.rsqrt(ssq_ref[...] * (1.0 / kdim) + EPS)
        y = alpha * h_ref[...] + gate_ref[...] * o
        mu = jnp.mean(y, axis=1, keepdims=True)
        yc = y - mu
        var = jnp.mean(yc * yc, axis=1, keepdims=True)
        hn = yc * lax.rsqrt(var + EPS) * lng_ref[...] + lnb_ref[...]
        hout_ref[...] = hn
        if emit_u:
            u_ref[...] = (hn * (1.0 + sc_ref[...]) + sh_ref[...]).astype(BF16)


def _matmul_ln(lay, x, w, h, mods, gate_layer, gate_blk, ln_g, ln_b, alpha, *,
               next_mod=None, rms_gain=None, tk):
    m, kdim = x.shape
    d = w.shape[1]
    tm = _pow2_tile(512, lay.seq, lay.m_ctx)
    nk = kdim // tk
    emit_u = next_mod is not None
    rms = rms_gain is not None

    def mod_spec(layer, blk):
        return pl.BlockSpec((None, None, 1, d), lambda i, k: (layer, lay.mod_row(i, tm), 0, blk))

    row_spec = pl.BlockSpec((1, d), lambda i, k: (0, 0))
    in_specs = [pl.BlockSpec((tm, tk), lambda i, k: (i, k)),
                pl.BlockSpec((tk, d), lambda i, k: (k, 0)),
                pl.BlockSpec((tm, d), lambda i, k: (i, 0)),
                mod_spec(gate_layer, gate_blk), row_spec, row_spec]
    args = [x, w, h, mods, ln_g.reshape(1, d), ln_b.reshape(1, d)]
    if emit_u:
        nl, sc_blk, sh_blk = next_mod
        in_specs += [mod_spec(nl, sc_blk), mod_spec(nl, sh_blk)]
        args += [mods, mods]
    if rms:
        in_specs.append(pl.BlockSpec((1, tk), lambda i, k: (0, k)))
        args.append(rms_gain.reshape(1, kdim))
    out_specs = [pl.BlockSpec((tm, d), lambda i, k: (i, 0))]
    out_shape = [jax.ShapeDtypeStruct((m, d), F32)]
    if emit_u:
        out_specs.append(pl.BlockSpec((tm, d), lambda i, k: (i, 0)))
        out_shape.append(jax.ShapeDtypeStruct((m, d), BF16))
    scratch = [pltpu.VMEM((tm, d), F32)]
    if rms:
        scratch.append(pltpu.VMEM((tm, 1), F32))
    outs = pl.pallas_call(
        functools.partial(_mm_ln_kernel, nk=nk, kdim=kdim, alpha=alpha, rms=rms, emit_u=emit_u),
        grid=(m // tm, nk),
        in_specs=in_specs, out_specs=out_specs, out_shape=out_shape, scratch_shapes=scratch,
        compiler_params=_cparams("parallel", "arbitrary"),
        name="out_proj_ln",
    )(*args)
    return (outs[0], outs[1]) if emit_u else (outs[0], None)


_EXT_PAD = 8


def _ffn_up_kernel(xp_ref, x_ref, xn_ref, wg_ref, wu_ref, cw_ref, o_ref, ext_ref, up_ref, *,
                   tm, gw, rb, tiles_per_img, n_lat_tiles, ctx_len):
    i = pl.program_id(1)
    is_lat = i < n_lat_tiles
    t_img = i % tiles_per_img
    has_prev = jnp.logical_and(is_lat, t_img > 0).astype(F32)
    has_next = jnp.logical_and(is_lat, t_img < tiles_per_img - 1).astype(F32)
    vert = is_lat.astype(F32)
    period = jnp.where(is_lat, gw, ctx_len)
    tn = o_ref.shape[1]
    base = _EXT_PAD + gw

    wg = wg_ref[...]
    zpad = jnp.zeros((_EXT_PAD, tn), F32)
    ext_ref[0:_EXT_PAD, :] = zpad
    ext_ref[_EXT_PAD:base, :] = _dot(xp_ref[...], wg) * has_prev
    ext_ref[base:base + tm, :] = _dot(x_ref[...], wg)
    ext_ref[base + tm:base + tm + gw, :] = _dot(xn_ref[...], wg) * has_next
    ext_ref[base + tm + gw:base + tm + gw + _EXT_PAD, :] = zpad
    up_ref[...] = _dot(x_ref[...], wu_ref[...])

    cw = cw_ref[...]
    taps = [[cw[3 * a + b:3 * a + b + 1, :] * (1.0 if a == 1 else vert) for b in range(3)]
            for a in range(3)]
    bias = cw[9:10, :]

    for blk in range(tm // rb):
        r0 = blk * rb
        col = (lax.broadcasted_iota(jnp.int32, (rb, tn), 0) + (i * tm + r0)) & (period - 1)
        first = col == 0
        last = col == period - 1
        acc = jnp.zeros((rb, tn), F32) + bias
        for a in range(3):
            off = base + r0 + (a - 1) * gw
            left = jnp.where(first, 0.0, ext_ref[off - 1:off - 1 + rb, :])
            mid = ext_ref[off:off + rb, :]
            right = jnp.where(last, 0.0, ext_ref[off + 1:off + 1 + rb, :])
            acc = acc + taps[a][0] * left + taps[a][1] * mid + taps[a][2] * right
        o_ref[r0:r0 + rb, :] = (_silu(acc) * up_ref[r0:r0 + rb, :]).astype(BF16)


def _ffn_up(lay, u, wg, wu, cw):
    m, d = u.shape
    hp = wg.shape[1]
    tm = _pow2_tile(1024, lay.seq, lay.m_ctx)
    tn = 512 if hp % 512 == 0 else 256
    gw = GRID_W
    rpt = tm // gw
    n_lat_tiles = lay.m_lat // tm
    last_halo = m // gw - 1
    kern = functools.partial(_ffn_up_kernel, tm=tm, gw=gw, rb=min(tm, 128),
                             tiles_per_img=lay.seq // tm, n_lat_tiles=n_lat_tiles, ctx_len=lay.ctx)
    return pl.pallas_call(
        kern,
        grid=(hp // tn, m // tm),
        in_specs=[pl.BlockSpec((gw, d), lambda j, i: (jnp.maximum(i * rpt - 1, 0), 0)),
                  pl.BlockSpec((tm, d), lambda j, i: (i, 0)),
                  pl.BlockSpec((gw, d), lambda j, i: (jnp.minimum((i + 1) * rpt, last_halo), 0)),
                  pl.BlockSpec((d, tn), lambda j, i: (0, j)),
                  pl.BlockSpec((d, tn), lambda j, i: (0, j)),
                  pl.BlockSpec((16, tn), lambda j, i: (0, j))],
        out_specs=pl.BlockSpec((tm, tn), lambda j, i: (i, j)),
        out_shape=jax.ShapeDtypeStruct((m, hp), BF16),
        scratch_shapes=[pltpu.VMEM((tm + 2 * gw + 2 * _EXT_PAD, tn), F32), pltpu.VMEM((tm, tn), F32)],
        compiler_params=_cparams("parallel", "parallel"),
        name="ffn_up_conv",
    )(u, u, u, wg, wu, cw)


def _seq_conv_kernel(xp_ref, x_ref, xn_ref, cw_ref, o_ref, ext_ref, *, tm, n_lat_tiles, seq, ctx_len):
    i = pl.program_id(1)
    period = jnp.where(i < n_lat_tiles, seq, ctx_len)
    tn = o_ref.shape[1]
    ext_ref[0:8, :] = xp_ref[...].astype(F32)
    ext_ref[8:8 + tm, :] = x_ref[...].astype(F32)
    ext_ref[8 + tm:16 + tm, :] = xn_ref[...].astype(F32)
    pos = (lax.broadcasted_iota(jnp.int32, (tm, tn), 0) + i * tm) & (period - 1)
    cw = cw_ref[...]
    left = jnp.where(pos == 0, 0.0, ext_ref[7:7 + tm, :])
    right = jnp.where(pos == period - 1, 0.0, ext_ref[9:9 + tm, :])
    y = cw[0:1, :] * left + cw[1:2, :] * ext_ref[8:8 + tm, :] + cw[2:3, :] * right + cw[3:4, :]
    o_ref[...] = _silu(y).astype(BF16)


def _seq_conv(lay, proj, col0, width, cw):
    m = proj.shape[0]
    tm = _pow2_tile(256, lay.seq, lay.ctx)
    tn = 1024
    cb0 = col0 // tn
    tb = tm // 8
    last8 = m // 8 - 1
    kern = functools.partial(_seq_conv_kernel, tm=tm, n_lat_tiles=lay.m_lat // tm, seq=lay.seq,
                             ctx_len=lay.ctx)
    return pl.pallas_call(
        kern,
        grid=(width // tn, m // tm),
        in_specs=[pl.BlockSpec((8, tn), lambda j, i: (jnp.maximum(i * tb - 1, 0), cb0 + j)),
                  pl.BlockSpec((tm, tn), lambda j, i: (i, cb0 + j)),
                  pl.BlockSpec((8, tn), lambda j, i: (jnp.minimum((i + 1) * tb, last8), cb0 + j)),
                  pl.BlockSpec((8, tn), lambda j, i: (0, j))],
        out_specs=pl.BlockSpec((tm, tn), lambda j, i: (i, j)),
        out_shape=jax.ShapeDtypeStruct((m, width), BF16),
        scratch_shapes=[pltpu.VMEM((tm + 16, tn), F32)],
        compiler_params=_cparams("parallel", "parallel"),
        name="ssd_conv",
    )(proj, proj, proj, cw)


def _scan_blocks(lay, q, reverse):
    nctx, nlat = lay.ctx // q, lay.seq // q
    ctx0 = lay.m_lat // q

    def row_block(b, s):
        cj = (nctx - 1 - s) if reverse else s
        lj = (nlat - 1 - (s - nctx)) if reverse else (s - nctx)
        return jnp.where(s < nctx, ctx0 + b * nctx + cj, b * nlat + lj)

    def pos_block(b, s):
        if not reverse:
            return s
        return jnp.where(s < nctx, nctx - 1 - s, nctx + nlat - 1 - (s - nctx))

    return nctx + nlat, row_block, pos_block


def _cumsum_rows(x):
    n = x.shape[0]
    rows = lax.broadcasted_iota(jnp.int32, x.shape, 0)
    shift = 1
    while shift < n:
        x = x + jnp.where(rows >= shift, pltpu.roll(x, shift, 0), 0.0)
        shift *= 2
    return x


def _tri_mask(q, reverse):
    ii = lax.broadcasted_iota(jnp.int32, (q, q), 0)
    jj = lax.broadcasted_iota(jnp.int32, (q, q), 1)
    return (ii <= jj) if reverse else (ii >= jj)


def _ssd_scan_kernel(*refs, q, reverse, n_pairs, final):
    it = iter(refs)
    xs_refs = (next(it), next(it))
    bc_ref, dt_ref, par_ref = next(it), next(it), next(it)
    yf_refs = z_refs = dsk_ref = None
    if final:
        yf_refs = (next(it), next(it))
        z_refs = (next(it), next(it))
        dsk_ref = next(it)
    out_refs = (next(it), next(it))
    s_ref = next(it)

    @pl.when(pl.program_id(1) == 0)
    def _():
        s_ref[...] = jnp.zeros_like(s_ref)

    lane0 = 64 if reverse else 0
    dt = _softplus(dt_ref[...].astype(F32) + par_ref[0:1, :])
    la = dt * par_ref[1:2, :]
    cum = _cumsum_rows(la)
    total = cum[q - 1:q, :]
    if reverse:
        cum = total - cum + la
    cum_t = cum.T
    dt_t = dt.T
    wout_t = (jnp.exp(total - cum) * dt).T
    e_in = jnp.exp(cum)
    e_tot = jnp.exp(total)
    mask = _tri_mask(q, reverse)
    lane = lax.broadcasted_iota(jnp.int32, (1, LANES), 1)
    lo_half = lane < SSD_HEAD_DIM
    lo_sel = jnp.where(lo_half, 1.0, 0.0).astype(BF16)
    hi_sel = jnp.where(lo_half, 0.0, 1.0).astype(BF16)
    half = n_pairs // 2
    pairs_per_group = n_pairs // SSD_GROUPS

    for g in range(SSD_GROUPS):
        k = bc_ref[:, g * SSD_STATE:(g + 1) * SSD_STATE]
        qm = bc_ref[:, (SSD_GROUPS + g) * SSD_STATE:(SSD_GROUPS + g + 1) * SSD_STATE]
        scores = _dot_nt(qm, k)
        k_t = k.astype(F32).T
        for p in range(pairs_per_group):
            pair = g * pairs_per_group + p
            xr = xs_refs[pair // half]
            c0 = (pair % half) * LANES
            xs = xr[:, c0:c0 + LANES]
            xs_half = (xs * lo_sel, xs * hi_sel)
            s_pair = s_ref[pair]
            y = jnp.zeros((q, LANES), F32)
            upd = jnp.zeros((SSD_STATE, LANES), F32)
            hs = (lane0 + 2 * pair, lane0 + 2 * pair + 1)
            for hh, xh in zip(hs, xs_half):
                decay = jnp.exp(jnp.where(mask, cum[:, hh:hh + 1] - cum_t[hh:hh + 1, :], NEG_BIG))
                wts = (scores * decay * dt_t[hh:hh + 1, :]).astype(BF16)
                y = y + _dot(wts, xh)
                upd = upd + _dot((k_t * wout_t[hh:hh + 1, :]).astype(BF16), xh)
            e_in_pair = jnp.where(lo_half, e_in[:, hs[0]:hs[0] + 1], e_in[:, hs[1]:hs[1] + 1])
            y = y + e_in_pair * _dot(qm, s_pair.astype(BF16))
            e_tot_pair = jnp.where(lo_half, e_tot[:, hs[0]:hs[0] + 1], e_tot[:, hs[1]:hs[1] + 1])
            s_ref[pair] = s_pair * e_tot_pair + upd
            o_ref = out_refs[pair // half]
            if final:
                y = y + yf_refs[pair // half][:, c0:c0 + LANES].astype(F32)
                y = y + dsk_ref[:, pair * LANES:(pair + 1) * LANES] * xs.astype(F32)
                y = y * _silu(z_refs[pair // half][:, c0:c0 + LANES].astype(F32))
            o_ref[:, c0:c0 + LANES] = y.astype(o_ref.dtype)


def _ssd_scan(lay, xbc, proj, par, dskip, d_inner, dt_col, reverse, yf):
    q = 128
    m = xbc.shape[0]
    nsteps, row_block, _ = _scan_blocks(lay, q, reverse)
    hw = d_inner // 2
    n_pairs = d_inner // LANES
    final = yf is not None
    tok = lambda cb: (lambda b, s: (row_block(b, s), cb))
    in_specs = [pl.BlockSpec((q, hw), tok(0)), pl.BlockSpec((q, hw), tok(1)),
                pl.BlockSpec((q, hw), tok(2)),
                pl.BlockSpec((q, LANES), tok(dt_col // LANES)),
                pl.BlockSpec((8, LANES), lambda b, s: (0, 0))]
    args = [xbc, xbc, xbc, proj, par]
    if final:
        in_specs += [pl.BlockSpec((q, hw), tok(0)), pl.BlockSpec((q, hw), tok(0)),
                     pl.BlockSpec((q, hw), tok(0)), pl.BlockSpec((q, hw), tok(1)),
                     pl.BlockSpec((1, d_inner), lambda b, s: (0, 0))]
        args += [yf[0], yf[1], proj, proj, dskip]
    kern = functools.partial(_ssd_scan_kernel, q=q, reverse=reverse, n_pairs=n_pairs, final=final)
    return pl.pallas_call(
        kern,
        grid=(lay.batch, nsteps),
        in_specs=in_specs,
        out_specs=[pl.BlockSpec((q, hw), tok(0)), pl.BlockSpec((q, hw), tok(0))],
        out_shape=[jax.ShapeDtypeStruct((m, hw), BF16), jax.ShapeDtypeStruct((m, hw), BF16)],
        scratch_shapes=[pltpu.VMEM((n_pairs, SSD_STATE, LANES), F32)],
        compiler_params=_cparams("parallel", "arbitrary"),
        name="ssd_scan_bwd" if reverse else "ssd_scan_fwd",
    )(*args)


def _ret_scan_kernel(*refs, q, reverse, hk, hv, final):
    it = iter(refs)
    q_ref, k_ref, v_ref, cos_ref, sin_ref, dl_ref = (next(it) for _ in range(6))
    yf_ref = g_ref = gng_ref = gnb_ref = None
    if final:
        yf_ref, g_ref, gng_ref, gnb_ref = (next(it) for _ in range(4))
    o_ref = next(it)
    s_ref, dec_ref, ein_ref, eout_ref = (next(it) for _ in range(4))
    half = hk // 2
    row = 1 if reverse else 0

    @pl.when(pl.program_id(1) == 0)
    def _():
        s_ref[...] = jnp.zeros_like(s_ref)

    @pl.when(pl.program_id(1) == 0)
    def _():
        ii = lax.broadcasted_iota(jnp.int32, (q, q), 0)
        jj = lax.broadcasted_iota(jnp.int32, (q, q), 1)
        dist = ((jj - ii) if reverse else (ii - jj)).astype(F32)
        mask = _tri_mask(q, reverse)
        pos = lax.broadcasted_iota(jnp.int32, (q, LANES), 0).astype(F32)
        for h in range(RET_HEADS):
            lg = _log_sigmoid(dl_ref[row:row + 1, h * LANES:(h + 1) * LANES])
            dec_ref[h] = jnp.exp(jnp.where(mask, dist * lg[:, :q], NEG_BIG))
            if reverse:
                ein_ref[h] = jnp.exp((q - pos) * lg)
                eout_ref[h] = jnp.exp(pos * lg)
            else:
                ein_ref[h] = jnp.exp((pos + 1.0) * lg)
                eout_ref[h] = jnp.exp((q - 1.0 - pos) * lg)

    cos = cos_ref[...]
    sin = sin_ref[...]

    def rot(ref, h):
        x0 = ref[:, h * hk:h * hk + half].astype(F32)
        x1 = ref[:, h * hk + half:(h + 1) * hk].astype(F32)
        return x0 * cos - x1 * sin, x0 * sin + x1 * cos

    scale = hk ** -0.5
    for h in range(RET_HEADS):
        q0, q1 = rot(q_ref, h)
        k0, k1 = rot(k_ref, h)
        k0, k1 = k0 * scale, k1 * scale
        qr = jnp.concatenate([q0, q1], axis=1).astype(BF16)
        kr = jnp.concatenate([k0, k1], axis=1).astype(BF16)
        v = v_ref[:, h * hv:(h + 1) * hv]
        scores = _dot_nt(qr, kr)
        y = _dot((scores * dec_ref[h]).astype(BF16), v)
        st = s_ref[h]
        ein = ein_ref[h]
        inter = _dot(qr, st.astype(BF16))
        y = y + jnp.concatenate([ein] * (hv // LANES), axis=1) * inter
        eout = eout_ref[h]
        ks = jnp.concatenate([k0 * eout, k1 * eout], axis=1).astype(BF16)
        e_tot = ein[0:1, :] * eout[0:1, :]
        s_ref[h] = st * jnp.concatenate([e_tot] * (hv // LANES), axis=1) + _dot_tn(ks, v)
        if final:
            y = y + yf_ref[:, h * hv:(h + 1) * hv].astype(F32)
            mu = jnp.mean(y, axis=1, keepdims=True)
            yc = y - mu
            var = jnp.mean(yc * yc, axis=1, keepdims=True)
            yn = yc * lax.rsqrt(var + EPS) * gng_ref[:, h * hv:(h + 1) * hv] + gnb_ref[:, h * hv:(h + 1) * hv]
            y = _silu(g_ref[:, h * hv:(h + 1) * hv].astype(F32)) * yn
        o_ref[:, h * hv:(h + 1) * hv] = y.astype(o_ref.dtype)


def _ret_scan(lay, proj, cos, sin, dl, gn_g, gn_b, qk_dim, v_dim, reverse, yf):
    q = 128
    m = proj.shape[0]
    nsteps, row_block, pos_block = _scan_blocks(lay, q, reverse)
    final = yf is not None
    hk, hv = qk_dim // RET_HEADS, v_dim // RET_HEADS
    tok = lambda w, cb: pl.BlockSpec((q, w), lambda b, s: (row_block(b, s), cb))
    pos = pl.BlockSpec((q, hk // 2), lambda b, s: (pos_block(b, s), 0))
    const = lambda shape: pl.BlockSpec(shape, lambda b, s: (0, 0))
    in_specs = [tok(qk_dim, 0), tok(qk_dim, 1), tok(v_dim, qk_dim * 2 // v_dim), pos, pos,
                const(dl.shape)]
    args = [proj, proj, proj, cos, sin, dl]
    if final:
        in_specs += [tok(v_dim, 0), tok(v_dim, qk_dim * 2 // v_dim + 1), const((1, v_dim)), const((1, v_dim))]
        args += [yf, proj, gn_g.reshape(1, v_dim), gn_b.reshape(1, v_dim)]
    kern = functools.partial(_ret_scan_kernel, q=q, reverse=reverse, hk=hk, hv=hv, final=final)
    return pl.pallas_call(
        kern,
        grid=(lay.batch, nsteps),
        in_specs=in_specs,
        out_specs=tok(v_dim, 0),
        out_shape=jax.ShapeDtypeStruct((m, v_dim), BF16),
        scratch_shapes=[pltpu.VMEM((RET_HEADS, hk, hv), F32), pltpu.VMEM((RET_HEADS, q, q), F32),
                        pltpu.VMEM((RET_HEADS, q, LANES), F32), pltpu.VMEM((RET_HEADS, q, LANES), F32)],
        compiler_params=_cparams("parallel", "arbitrary"),
        name="ret_scan_bwd" if reverse else "ret_scan_fwd",
    )(*args)


def _vscan_kernel(*refs, kind, q, reverse, heads, hk, hv, final):
    it = iter(refs)
    if kind == "gla":
        q_ref, k_ref, v_ref, a_ref, wup_ref, ab_ref = (next(it) for _ in range(6))
    else:
        q_ref, v_ref, f_ref, lb_ref = (next(it) for _ in range(4))
    yf_ref = gate_ref = ng_ref = None
    if final:
        yf_ref, gate_ref, ng_ref = (next(it) for _ in range(3))
    o_ref = next(it)
    st_ref = next(it)

    @pl.when(pl.program_id(1) == 0)
    def _():
        st_ref[...] = jnp.zeros_like(st_ref)

    if kind == "gla":
        logit = _dot(a_ref[...], wup_ref[...]) + ab_ref[...]
        la = _log_sigmoid(logit) * (1.0 / GLA_TAU)
        qf = q_ref[...].astype(F32) * (hk ** -0.5)
        kf = k_ref[...].astype(F32)
    else:
        lb = lb_ref[...]
        f = lb + (1.0 - lb) * _sigmoid(f_ref[...].astype(F32))
        la = jnp.log(f)
        kf = 1.0 - f
        qf = _silu(q_ref[...].astype(F32))

    cum = _cumsum_rows(la)
    total = cum[q - 1:q, :]
    if reverse:
        cum = total - cum + la
        ref = cum[q // 2 - 1:q // 2, :]
    else:
        ref = cum[q // 2:q // 2 + 1, :]
    q_rel = qf * jnp.exp(cum - ref)
    k_rel = kf * jnp.exp(ref - cum)
    q_abs = (q_rel * jnp.exp(ref)).astype(BF16)
    k_out = (k_rel * jnp.exp(total - ref)).astype(BF16)
    q_rel = q_rel.astype(BF16)
    k_rel = k_rel.astype(BF16)
    e_tot = jnp.exp(total)
    mask = _tri_mask(q, reverse)

    for h in range(heads):
        ks = slice(h * hk, (h + 1) * hk)
        vs = slice(h * hv, (h + 1) * hv)
        v = v_ref[:, vs]
        scores = jnp.where(mask, _dot_nt(q_rel[:, ks], k_rel[:, ks]), 0.0)
        st = st_ref[h]
        y = _dot(scores.astype(BF16), v) + _dot_nt(q_abs[:, ks], st.astype(BF16))
        st_ref[h] = st * e_tot[:, ks] + _dot_tn(v, k_out[:, ks])
        if final:
            y = y + yf_ref[:, vs].astype(F32)
            ms = jnp.mean(y * y, axis=1, keepdims=True)
            y = y * lax.rsqrt(ms + EPS) * ng_ref[...] * _silu(gate_ref[:, vs].astype(F32))
        o_ref[:, vs] = y.astype(o_ref.dtype)


def _vscan(lay, kind, proj, extra, norm_g, heads, hk, hv, reverse, yf):
    q = 64
    m = proj.shape[0]
    nsteps, row_block, _ = _scan_blocks(lay, q, reverse)
    final = yf is not None
    kd, vd = heads * hk, heads * hv
    tok = lambda w, cb: pl.BlockSpec((q, w), lambda b, s: (row_block(b, s), cb))
    const = lambda shape, cb: pl.BlockSpec(shape, lambda b, s: (0, cb))
    d = 1 if reverse else 0
    if kind == "gla":
        wup, ab = extra
        in_specs = [tok(kd, 0), tok(kd, 1), tok(vd, 2 * kd // vd), tok(LANES, (2 * kd + 2 * vd) // LANES),
                    const((LANES, kd), d), const((1, kd), d)]
        args = [proj, proj, proj, proj, wup, ab]
        gate_cb = 2 * kd // vd + 1
    else:
        (lb,) = extra
        in_specs = [tok(kd, 0), tok(vd, kd // vd), tok(kd, (kd + 2 * vd) // kd + d), const((1, kd), 0)]
        args = [proj, proj, proj, lb]
        gate_cb = kd // vd + 1
    if final:
        in_specs += [tok(vd, 0), tok(vd, gate_cb), const((1, hv), 0)]
        args += [yf, proj, norm_g.reshape(1, hv)]
    kern = functools.partial(_vscan_kernel, kind=kind, q=q, reverse=reverse, heads=heads, hk=hk, hv=hv,
                             final=final)
    return pl.pallas_call(
        kern,
        grid=(lay.batch, nsteps),
        in_specs=in_specs,
        out_specs=tok(vd, 0),
        out_shape=jax.ShapeDtypeStruct((m, vd), BF16),
        scratch_shapes=[pltpu.VMEM((heads, hv, hk), F32)],
        compiler_params=_cparams("parallel", "arbitrary"),
        name=f"{kind}_scan_{'bwd' if reverse else 'fwd'}",
    )(*args)


def _lower_bound_kernel(x_ref, o_ref):
    x = x_ref[...]
    mx = jnp.max(x, axis=0, keepdims=True)
    e = jnp.exp(x - mx)
    p = e / jnp.sum(e, axis=0, keepdims=True)
    acc = jnp.zeros_like(p[0:1])
    for i in range(x.shape[0]):
        acc = acc + p[i:i + 1]
        o_ref[i:i + 1, :] = acc - p[0:1]


def _lower_bounds(logits):
    return pl.pallas_call(_lower_bound_kernel, out_shape=jax.ShapeDtypeStruct(logits.shape, F32),
                          name="hgrn_lower_bounds")(logits)


def _in_proj_tiles(lay, n):
    tm = _pow2_tile(1024, lay.seq, lay.m_ctx)
    for tn in (1152, 1024, 896, 768, 640, 512, 384, 256, 128):
        if n % tn == 0:
            return tm, tn
    raise ValueError(f"projection width {n} is not a multiple of {LANES}")


def _ssd_mixer(lay, u, w_in, conv_w, conv_b, dt_bias, a_log, d_skip, d_model):
    d_inner = 2 * d_model
    heads = d_inner // SSD_HEAD_DIM
    conv_ch = d_inner + 2 * SSD_GROUPS * SSD_STATE
    tm, tn = _in_proj_tiles(lay, w_in.shape[1])
    proj = _matmul(u, w_in.astype(BF16), tm, tn)
    cw = jnp.concatenate([conv_w, conv_b[None], jnp.zeros((4, conv_ch), F32)], axis=0)
    xbc = _seq_conv(lay, proj, d_inner, conv_ch, cw)
    par = jnp.concatenate([dt_bias.reshape(1, 2 * heads), -jnp.exp(a_log.astype(F32)).reshape(1, 2 * heads),
                           jnp.zeros((6, 2 * heads), F32)], axis=0)
    dskip = jnp.repeat(d_skip, SSD_HEAD_DIM).reshape(1, d_inner)
    dt_col = d_inner + conv_ch
    yf = _ssd_scan(lay, xbc, proj, par, dskip, d_inner, dt_col, False, None)
    ya, yb = _ssd_scan(lay, xbc, proj, par, dskip, d_inner, dt_col, True, yf)
    return jnp.concatenate([ya, yb], axis=1)


def _deinterleave_heads(w, heads):
    d, n = w.shape
    hk = n // heads
    return w.reshape(d, heads, hk // 2, 2).transpose(0, 1, 3, 2).reshape(d, n)


def _ret_mixer(lay, u, w_in, decay_logit, gn_g, gn_b, d_model):
    qk, vd = d_model, 2 * d_model
    hk = qk // RET_HEADS
    w = jnp.concatenate([_deinterleave_heads(w_in[:, :qk], RET_HEADS),
                         _deinterleave_heads(w_in[:, qk:2 * qk], RET_HEADS), w_in[:, 2 * qk:]], axis=1)
    tm, tn = _in_proj_tiles(lay, w.shape[1])
    proj = _matmul(u, w.astype(BF16), tm, tn)
    half = hk // 2
    inv_freq = 1.0 / (10000.0 ** jnp.linspace(0.0, 1.0, half, dtype=F32))
    ang = jnp.arange(lay.ctx + lay.seq, dtype=F32)[:, None] * inv_freq[None, :]
    cos, sin = jnp.cos(ang), jnp.sin(ang)
    dl = jnp.repeat(decay_logit.astype(F32), LANES, axis=1)
    dl = jnp.concatenate([dl, jnp.zeros((6, dl.shape[1]), F32)], axis=0)
    yf = _ret_scan(lay, proj, cos, sin, dl, gn_g, gn_b, qk, vd, False, None)
    return _ret_scan(lay, proj, cos, sin, dl, gn_g, gn_b, qk, vd, True, yf)


def _gla_mixer(lay, u, w_in, w_alpha_up, alpha_b, norm_g, d_model):
    kd, vd = d_model // 2, d_model
    n_main = 2 * kd + 2 * vd
    w = jnp.concatenate([w_in, jnp.zeros((d_model, LANES - 2 * GLA_RANK), F32)], axis=1)
    tm, tn = _in_proj_tiles(lay, w.shape[1])
    proj = _matmul(u, w.astype(BF16), tm, tn)
    wup = jnp.zeros((LANES, 2 * kd), F32)
    wup = wup.at[0:GLA_RANK, 0:kd].set(w_alpha_up[0]).at[GLA_RANK:2 * GLA_RANK, kd:].set(w_alpha_up[1])
    ab = alpha_b.reshape(1, 2 * kd)
    del n_main
    args = (lay, "gla", proj, (wup.astype(BF16), ab), norm_g, GLA_HEADS, kd // GLA_HEADS, vd // GLA_HEADS)
    yf = _vscan(*args, False, None)
    return _vscan(*args, True, yf)


def _hgrn_mixer(lay, u, w_in, lower_bound, norm_g, d_model):
    heads = d_model // HGRN_HEAD_K
    tm, tn = _in_proj_tiles(lay, w_in.shape[1])
    proj = _matmul(u, w_in.astype(BF16), tm, tn)
    args = (lay, "hgrn", proj, (lower_bound.reshape(1, d_model),), norm_g, heads, HGRN_HEAD_K,
            d_model // heads)
    yf = _vscan(*args, False, None)
    return _vscan(*args, True, yf)


def kernel(x, c, ctx, c_ctx, mod_w, mod_b, ln_mix_g, ln_mix_b, ln_ffn_g, ln_ffn_b, ffn_w_up, ffn_conv_w, ffn_conv_b, ffn_w_down, hgrn_lb_logits, ssd_w_in, ssd_conv_w, ssd_conv_b, ssd_dt_bias, ssd_a_log, ssd_d, ssd_norm_g, ssd_w_out, ret_w_in, ret_decay_logit, ret_gn_g, ret_gn_b, ret_w_out, gla_w_in, gla_w_alpha_up, gla_alpha_b, gla_norm_g, gla_w_out, hgrn_w_in, hgrn_norm_g, hgrn_w_out):
    batch, seq, d = x.shape
    ctx_len = ctx.shape[1]
    depth = mod_w.shape[0]
    hidden = ffn_w_down.shape[1]
    lay = _Layout(batch, seq, ctx_len)
    alpha = (2.0 * depth) ** 0.25
    n_mixers = 4

    c_rows = jnp.concatenate([c, c_ctx[None], jnp.zeros((8 - batch - 1, d), F32)], axis=0)
    mods = _modulation(c_rows, mod_w, mod_b).reshape(depth, 8, 1, 6 * d)
    SH_M, SC_M, G_M, SH_F, SC_F, G_F = range(6)

    lower_bounds = _lower_bounds(hgrn_lb_logits.astype(F32))

    h = jnp.concatenate([x.reshape(lay.m_lat, d), ctx.reshape(lay.m_ctx, d)], axis=0)
    u = _modulate(lay, h, mods, 0, SC_M, SH_M)

    hp = -(-hidden // 512) * 512
    for i in range(depth):
        kind, j = i % n_mixers, i // n_mixers
        rms_gain = None
        if kind == 0:
            y = _ssd_mixer(lay, u, ssd_w_in[j], ssd_conv_w[j], ssd_conv_b[j], ssd_dt_bias[j], ssd_a_log[j],
                           ssd_d[j], d)
            w_out, rms_gain = ssd_w_out[j], ssd_norm_g[j]
        elif kind == 1:
            y = _ret_mixer(lay, u, ret_w_in[j], ret_decay_logit[j], ret_gn_g[j], ret_gn_b[j], d)
            w_out = ret_w_out[j]
        elif kind == 2:
            y = _gla_mixer(lay, u, gla_w_in[j], gla_w_alpha_up[j], gla_alpha_b[j], gla_norm_g[j], d)
            w_out = gla_w_out[j]
        else:
            y = _hgrn_mixer(lay, u, hgrn_w_in[j], lower_bounds[i], hgrn_norm_g[j], d)
            w_out = hgrn_w_out[j]
        kd = w_out.shape[0]
        h, u = _matmul_ln(lay, y, w_out.astype(BF16), h, mods, i, G_M, ln_mix_g[i], ln_mix_b[i], alpha,
                          next_mod=(i, SC_F, SH_F), rms_gain=rms_gain, tk=min(kd, 2048))

        pad = hp - hidden
        wg = jnp.pad(ffn_w_up[i][:, :hidden], ((0, 0), (0, pad))).astype(BF16)
        wu = jnp.pad(ffn_w_up[i][:, hidden:], ((0, 0), (0, pad))).astype(BF16)
        cw = jnp.concatenate([ffn_conv_w[i].reshape(9, hidden), ffn_conv_b[i][None],
                              jnp.zeros((6, hidden), F32)], axis=0)
        cw = jnp.pad(cw, ((0, 0), (0, pad)))
        wd = jnp.pad(ffn_w_down[i], ((0, pad), (0, 0))).astype(BF16)
        act = _ffn_up(lay, u, wg, wu, cw)
        nxt = (i + 1, SC_M, SH_M) if i + 1 < depth else None
        h, u = _matmul_ln(lay, act, wd, h, mods, i, G_F, ln_ffn_g[i], ln_ffn_b[i], alpha,
                          next_mod=nxt, tk=512)
    return h[:lay.m_lat].reshape(batch, seq, d)
```

```python
import functools

import jax
import jax.numpy as jnp
from jax import lax
from jax.experimental import pallas as pl
from jax.experimental.pallas import tpu as pltpu

F32 = jnp.float32
BF16 = jnp.bfloat16

EPS = 1e-5
GRID_W = 64
SSD_HEAD_DIM = 64
SSD_GROUPS = 8
SSD_STATE = 128
RET_HEADS = 8
GLA_HEADS = 4
GLA_RANK = 16
GLA_TAU = 16.0
HGRN_HEAD_K = 128
LANES = 128
SUBLANES = 8
VMEM_LIMIT = 56 * 1024 * 1024
NEG_BIG = -1e30


def _cparams(*sem):
    return pltpu.CompilerParams(dimension_semantics=sem, vmem_limit_bytes=VMEM_LIMIT)


def _sigmoid(x):
    return 1.0 / (1.0 + jnp.exp(-x))


def _silu(x):
    return x * _sigmoid(x)


def _softplus(x):
    return jnp.maximum(x, 0.0) + jnp.log(1.0 + jnp.exp(-jnp.abs(x)))


def _log_sigmoid(x):
    return jnp.minimum(x, 0.0) - jnp.log(1.0 + jnp.exp(-jnp.abs(x)))


def _pow2_tile(limit, *sizes):
    t = 1
    while t * 2 <= limit and all(s % (t * 2) == 0 for s in sizes):
        t *= 2
    return t


def _dot(a, b):
    return jnp.dot(a, b, preferred_element_type=F32)


def _dot_nt(a, b):
    return lax.dot_general(a, b, (((1,), (1,)), ((), ())), preferred_element_type=F32)


def _dot_tn(a, b):
    return lax.dot_general(a, b, (((0,), (0,)), ((), ())), preferred_element_type=F32)


class _Layout:
    def __init__(self, batch, seq, ctx):
        self.batch, self.seq, self.ctx = batch, seq, ctx
        self.m_lat = batch * seq
        self.m_ctx = batch * ctx
        self.m = self.m_lat + self.m_ctx
        self.tm = _pow2_tile(1024, seq, self.m_ctx)
        self.tm_ln = _pow2_tile(512, seq, self.m_ctx)

    def mod_row(self, tile, tm):
        start = tile * tm
        return jnp.where(start < self.m_lat, start // self.seq, self.batch)


def _mod_kernel(c_ref, w_ref, b_ref, o_ref):
    c = _silu(c_ref[...]).astype(BF16)
    o_ref[...] = _dot(c, w_ref[...].astype(BF16)) + b_ref[...]


def _modulation(c_rows, mod_w, mod_b):
    depth, d, n = mod_w.shape
    rows = c_rows.shape[0]
    tn = _pow2_tile(1024, n)
    return pl.pallas_call(
        _mod_kernel,
        grid=(depth, n // tn),
        in_specs=[pl.BlockSpec((rows, d), lambda l, j: (0, 0)),
                  pl.BlockSpec((None, d, tn), lambda l, j: (l, 0, j)),
                  pl.BlockSpec((None, 1, tn), lambda l, j: (l, 0, j))],
        out_specs=pl.BlockSpec((None, rows, tn), lambda l, j: (l, 0, j)),
        out_shape=jax.ShapeDtypeStruct((depth, rows, n), F32),
        compiler_params=_cparams("parallel", "parallel"),
        name="modulation",
    )(c_rows, mod_w, mod_b.reshape(depth, 1, n))


def _modulate_kernel(x_ref, c_ref, sc_ref, sh_ref, h_ref, u_ref, *, n_lat_tiles):
    def emit(src):
        h = src[...]
        h_ref[...] = h
        u_ref[...] = (h * (1.0 + sc_ref[...]) + sh_ref[...]).astype(BF16)

    pl.when(pl.program_id(0) < n_lat_tiles)(lambda: emit(x_ref))
    pl.when(pl.program_id(0) >= n_lat_tiles)(lambda: emit(c_ref))


def _modulate(lay, x2, c2, mods, layer, sc_blk, sh_blk):
    d = x2.shape[1]
    tm = lay.tm_ln
    nl = lay.m_lat // tm
    nc = lay.m_ctx // tm
    mod_spec = lambda blk: pl.BlockSpec((None, None, 1, d), lambda i: (layer, lay.mod_row(i, tm), 0, blk))
    tok = pl.BlockSpec((tm, d), lambda i: (i, 0))
    return pl.pallas_call(
        functools.partial(_modulate_kernel, n_lat_tiles=nl),
        grid=(nl + nc,),
        in_specs=[pl.BlockSpec((tm, d), lambda i: (jnp.minimum(i, nl - 1), 0)),
                  pl.BlockSpec((tm, d), lambda i: (jnp.maximum(i - nl, 0), 0)),
                  mod_spec(sc_blk), mod_spec(sh_blk)],
        out_specs=[tok, tok],
        out_shape=[jax.ShapeDtypeStruct((lay.m, d), F32), jax.ShapeDtypeStruct((lay.m, d), BF16)],
        compiler_params=_cparams("parallel"),
        name="modulate",
    )(x2, c2, mods, mods)


def _mm_kernel(x_ref, w_ref, o_ref, *scratch, cast_w):
    if cast_w:
        wb_ref, = scratch

        @pl.when(pl.program_id(1) == 0)
        def _():
            wb_ref[...] = w_ref[...].astype(BF16)

        w = wb_ref[...]
    else:
        w = w_ref[...]
    o_ref[...] = _dot(x_ref[...], w).astype(o_ref.dtype)


def _matmul(lay, x, w, col0, n, out_dtype=BF16):
    m, k = x.shape
    tm = lay.tm
    tn = _pow2_tile(1024, n, col0) if col0 else _pow2_tile(1024, n)
    cast_w = w.dtype != BF16
    cb0 = col0 // tn
    return pl.pallas_call(
        functools.partial(_mm_kernel, cast_w=cast_w),
        grid=(n // tn, m // tm),
        in_specs=[pl.BlockSpec((tm, k), lambda j, i: (i, 0)),
                  pl.BlockSpec((k, tn), lambda j, i: (0, cb0 + j))],
        out_specs=pl.BlockSpec((tm, tn), lambda j, i: (i, j)),
        out_shape=jax.ShapeDtypeStruct((m, n), out_dtype),
        scratch_shapes=[pltpu.VMEM((k, tn), BF16)] if cast_w else [],
        compiler_params=_cparams("parallel", "arbitrary"),
        name="in_proj",
    )(x, w)


def _mm_ln_kernel(*refs, nk, n_tiles, kdim, alpha, rms, emit_u):
    it = iter(refs)
    x_ref, w_ref, h_ref, gate_ref, lng_ref, lnb_ref = (next(it) for _ in range(6))
    sc_ref = sh_ref = rg_ref = u_ref = ssq_ref = None
    if emit_u:
        sc_ref, sh_ref = next(it), next(it)
    if rms:
        rg_ref = next(it)
    hout_ref = next(it)
    if emit_u:
        u_ref = next(it)
    acc_refs = (next(it), next(it))
    ssq_refs = (next(it), next(it)) if rms else (None, None)
    i = pl.program_id(0)
    k = pl.program_id(1)

    def product():
        x = x_ref[...]
        ssq = None
        if rms:
            xf = x.astype(F32)
            ssq = jnp.sum(xf * xf, axis=1, keepdims=True)
            x = (xf * rg_ref[...]).astype(BF16)
        return _dot(x, w_ref[...]), ssq

    @pl.when(jnp.logical_and(i == 0, k == 0))
    def _():
        acc_refs[1][...] = jnp.zeros_like(acc_refs[1])
        if rms:
            ssq_refs[1][...] = jnp.zeros_like(ssq_refs[1])

    for slot in (0, 1):
        acc_ref, ssq_ref = acc_refs[slot], ssq_refs[slot]
        prev_acc, prev_ssq = acc_refs[1 - slot], ssq_refs[1 - slot]

        @pl.when(jnp.logical_and(k == 0, i % 2 == slot))
        def _(acc_ref=acc_ref, ssq_ref=ssq_ref, prev_acc=prev_acc, prev_ssq=prev_ssq):
            o_new, ssq_new = product()
            acc_ref[...] = o_new
            if rms:
                ssq_ref[...] = ssq_new
            o = prev_acc[...]
            if rms:
                o = o * lax.rsqrt(prev_ssq[...] * (1.0 / kdim) + EPS)
            y = alpha * h_ref[...] + gate_ref[...] * o
            mu = jnp.mean(y, axis=1, keepdims=True)
            yc = y - mu
            var = jnp.mean(yc * yc, axis=1, keepdims=True)
            hn = yc * lax.rsqrt(var + EPS) * lng_ref[...] + lnb_ref[...]
            hout_ref[...] = hn
            if emit_u:
                u_ref[...] = (hn * (1.0 + sc_ref[...]) + sh_ref[...]).astype(BF16)

        if nk > 1:
            @pl.when(jnp.logical_and(jnp.logical_and(k > 0, i < n_tiles), i % 2 == slot))
            def _(acc_ref=acc_ref, ssq_ref=ssq_ref):
                o_new, ssq_new = product()
                acc_ref[...] += o_new
                if rms:
                    ssq_ref[...] += ssq_new


def _matmul_ln(lay, x, w, h, mods, gate_layer, gate_blk, ln_g, ln_b, alpha, *,
               next_mod=None, rms_gain=None, tk, m_rows=None):
    kdim = x.shape[1]
    m = x.shape[0] if m_rows is None else m_rows
    d = w.shape[1]
    tm = lay.tm_ln
    nk = kdim // tk
    n_tiles = m // tm
    emit_u = next_mod is not None
    rms = rms_gain is not None
    cur = lambda i: jnp.minimum(i, n_tiles - 1)
    prev = lambda i: jnp.maximum(i - 1, 0)

    def mod_spec(layer, blk):
        return pl.BlockSpec((None, None, 1, d), lambda i, k: (layer, lay.mod_row(prev(i), tm), 0, blk))

    row_spec = pl.BlockSpec((1, d), lambda i, k: (0, 0))
    in_specs = [pl.BlockSpec((tm, tk), lambda i, k: (cur(i), k)),
                pl.BlockSpec((tk, d), lambda i, k: (k, 0)),
                pl.BlockSpec((tm, d), lambda i, k: (prev(i), 0)),
                mod_spec(gate_layer, gate_blk), row_spec, row_spec]
    args = [x, w, h, mods, ln_g.reshape(1, d), ln_b.reshape(1, d)]
    if emit_u:
        nl, sc_blk, sh_blk = next_mod
        in_specs += [mod_spec(nl, sc_blk), mod_spec(nl, sh_blk)]
        args += [mods, mods]
    if rms:
        in_specs.append(pl.BlockSpec((1, tk), lambda i, k: (0, k)))
        args.append(rms_gain.reshape(1, kdim))
    out_specs = [pl.BlockSpec((tm, d), lambda i, k: (prev(i), 0))]
    out_shape = [jax.ShapeDtypeStruct((m, d), F32)]
    if emit_u:
        out_specs.append(pl.BlockSpec((tm, d), lambda i, k: (prev(i), 0)))
        out_shape.append(jax.ShapeDtypeStruct((m, d), BF16))
    scratch = [pltpu.VMEM((tm, d), F32), pltpu.VMEM((tm, d), F32)]
    if rms:
        scratch += [pltpu.VMEM((tm, 1), F32), pltpu.VMEM((tm, 1), F32)]
    outs = pl.pallas_call(
        functools.partial(_mm_ln_kernel, nk=nk, n_tiles=n_tiles, kdim=kdim, alpha=alpha, rms=rms,
                          emit_u=emit_u),
        grid=(n_tiles + 1, nk),
        in_specs=in_specs, out_specs=out_specs, out_shape=out_shape, scratch_shapes=scratch,
        compiler_params=_cparams("arbitrary", "arbitrary"),
        name="out_proj_ln",
    )(*args)
    return (outs[0], outs[1]) if emit_u else (outs[0], None)


_EXT_PAD = SUBLANES


def _proj_conv_kernel(*refs, tm, hr, vertical, has_up, cast_w, rb, cc, tiles_per_img, n_lat_tiles,
                      lat_period, ctx_len):
    it = iter(refs)
    xp_ref, x_ref, xn_ref, wg_ref = (next(it) for _ in range(4))
    wu_ref = next(it) if has_up else None
    cw_ref, o_ref, ext_ref = next(it), next(it), next(it)
    up_ref = next(it) if has_up else None
    wb_ref = next(it) if cast_w else None
    i = pl.program_id(1)
    if cast_w:
        @pl.when(i == 0)
        def _():
            wb_ref[...] = wg_ref[...].astype(BF16)

        wg_src = wb_ref
    else:
        wg_src = wg_ref
    is_lat = i < n_lat_tiles
    period = jnp.where(is_lat, lat_period, ctx_len)
    tn = o_ref.shape[1]
    base = _EXT_PAD + hr
    vert = is_lat.astype(F32)
    tap_rows = (0, 1, 2) if vertical else (1,)
    if vertical:
        t_img = i % tiles_per_img
        has_prev = jnp.logical_and(is_lat, t_img > 0).astype(F32)
        has_next = jnp.logical_and(is_lat, t_img < tiles_per_img - 1).astype(F32)

    for ci, c0 in enumerate(range(0, tn, cc)):
        cs = slice(c0, c0 + cc)
        ext = ext_ref.at[ci]
        wg = wg_src[:, cs]
        prev = _dot(xp_ref[...], wg)
        nxt = _dot(xn_ref[...], wg)
        if vertical:
            prev = prev * has_prev
            nxt = nxt * has_next
        zpad = jnp.zeros((_EXT_PAD, cc), F32)
        ext[0:_EXT_PAD, :] = zpad
        ext[_EXT_PAD:base, :] = prev
        ext[base:base + tm, :] = _dot(x_ref[...], wg)
        ext[base + tm:base + tm + hr, :] = nxt
        ext[base + tm + hr:base + tm + hr + _EXT_PAD, :] = zpad
        if has_up:
            up = up_ref.at[ci]
            up[...] = _dot(x_ref[...], wu_ref[:, cs])

        cw = cw_ref[:, cs]
        taps = {a: [cw[3 * a + b:3 * a + b + 1, :] * (1.0 if a == 1 else vert) for b in range(3)]
                for a in tap_rows}
        bias = cw[9:10, :]

        for blk in range(tm // rb):
            r0 = blk * rb
            col = (lax.broadcasted_iota(jnp.int32, (rb, cc), 0) + (i * tm + r0)) & (period - 1)
            first = col == 0
            last = col == period - 1
            acc = jnp.zeros((rb, cc), F32) + bias
            for a in tap_rows:
                off = base + r0 + (a - 1) * hr
                left = jnp.where(first, 0.0, ext[off - 1:off - 1 + rb, :])
                mid = ext[off:off + rb, :]
                right = jnp.where(last, 0.0, ext[off + 1:off + 1 + rb, :])
                acc = acc + taps[a][0] * left + taps[a][1] * mid + taps[a][2] * right
            act = _silu(acc)
            if has_up:
                act = act * up[r0:r0 + rb, :]
            o_ref[r0:r0 + rb, cs] = act.astype(BF16)


def _proj_conv(lay, u, wg, wu, cw, *, col0, n, tn, hr, vertical, lat_period, m_rows=None, name):
    d = u.shape[1]
    m = u.shape[0] if m_rows is None else m_rows
    tm = lay.tm
    hpt = tm // hr
    last_halo = u.shape[0] // hr - 1
    has_up = wu is not None
    cast_w = wg.dtype != BF16
    cb0 = col0 // tn
    cc = tn
    kern = functools.partial(_proj_conv_kernel, tm=tm, hr=hr, vertical=vertical, has_up=has_up,
                             cast_w=cast_w, rb=min(tm, 128), cc=cc, tiles_per_img=lay.seq // tm,
                             n_lat_tiles=lay.m_lat // tm, lat_period=lat_period, ctx_len=lay.ctx)
    w_spec = pl.BlockSpec((d, tn), lambda j, i: (0, cb0 + j))
    in_specs = [pl.BlockSpec((hr, d), lambda j, i: (jnp.maximum(i * hpt - 1, 0), 0)),
                pl.BlockSpec((tm, d), lambda j, i: (i, 0)),
                pl.BlockSpec((hr, d), lambda j, i: (jnp.minimum((i + 1) * hpt, last_halo), 0)),
                w_spec]
    args = [u, u, u, wg]
    if has_up:
        in_specs.append(w_spec)
        args.append(wu)
    in_specs.append(pl.BlockSpec((16, tn), lambda j, i: (0, j)))
    args.append(cw)
    scratch =[pltpu.VMEM((tn // cc, tm + 2 * hr + 2 * _EXT_PAD, cc), F32)]
    if has_up:
        scratch.append(pltpu.VMEM((tn // cc, tm, cc), F32))
    if cast_w:
        scratch.append(pltpu.VMEM((d, tn), BF16))
    return pl.pallas_call(
        kern,
        grid=(n // tn, m // tm),
        in_specs=in_specs,
        out_specs=pl.BlockSpec((tm, tn), lambda j, i: (i, j)),
        out_shape=jax.ShapeDtypeStruct((m, n), BF16),
        scratch_shapes=scratch,
        compiler_params=_cparams("parallel", "arbitrary"),
        name=name,
    )(*args)


def _scan_blocks(lay, q, reverse):
    nctx, nlat = lay.ctx // q, lay.seq // q
    ctx0 = lay.m_lat // q

    def row_block(b, s):
        cj = (nctx - 1 - s) if reverse else s
        lj = (nlat - 1 - (s - nctx)) if reverse else (s - nctx)
        return jnp.where(s < nctx, ctx0 + b * nctx + cj, b * nlat + lj)

    def pos_block(b, s):
        if not reverse:
            return s
        return jnp.where(s < nctx, nctx - 1 - s, nctx + nlat - 1 - (s - nctx))

    return nctx + nlat, row_block, pos_block


def _cumsum_rows(x):
    n = x.shape[0]
    rows = lax.broadcasted_iota(jnp.int32, x.shape, 0)
    shift = 1
    while shift < n:
        x = x + jnp.where(rows >= shift, pltpu.roll(x, shift, 0), 0.0)
        shift *= 2
    return x


def _tri_mask(q, reverse):
    ii = lax.broadcasted_iota(jnp.int32, (q, q), 0)
    jj = lax.broadcasted_iota(jnp.int32, (q, q), 1)
    return (ii <= jj) if reverse else (ii >= jj)


def _ssd_scan_kernel(*refs, q, reverse, n_pairs, final):
    it = iter(refs)
    xs_refs = (next(it), next(it))
    bc_ref, dt_ref, par_ref = next(it), next(it), next(it)
    yf_refs = z_refs = dsk_ref = None
    if final:
        yf_refs = (next(it), next(it))
        z_refs = (next(it), next(it))
        dsk_ref = next(it)
    out_refs = (next(it), next(it))
    s_ref = next(it)

    @pl.when(pl.program_id(1) == 0)
    def _():
        s_ref[...] = jnp.zeros_like(s_ref)

    lane0 = 64 if reverse else 0
    dt = _softplus(dt_ref[...] + par_ref[0:1, :])
    la = dt * par_ref[1:2, :]
    cum = _cumsum_rows(la)
    total = cum[q - 1:q, :]
    if reverse:
        cum = total - cum + la
    cum_t = cum.T
    dt_t = dt.T
    wout_t = (jnp.exp(total - cum) * dt).T
    e_tot = jnp.exp(total)
    mask = _tri_mask(q, reverse)
    lane = lax.broadcasted_iota(jnp.int32, (1, LANES), 1)
    lo_half = lane < SSD_HEAD_DIM
    lo_sel = jnp.where(lo_half, 1.0, 0.0).astype(BF16)
    hi_sel = jnp.where(lo_half, 0.0, 1.0).astype(BF16)
    half = n_pairs // 2
    pairs_per_group = n_pairs // SSD_GROUPS

    for g in range(SSD_GROUPS):
        k = bc_ref[:, g * SSD_STATE:(g + 1) * SSD_STATE]
        qm = bc_ref[:, (SSD_GROUPS + g) * SSD_STATE:(SSD_GROUPS + g + 1) * SSD_STATE]
        scores = _dot_nt(qm, k)
        k_t = k.astype(F32).T
        for p in range(pairs_per_group):
            pair = g * pairs_per_group + p
            xr = xs_refs[pair // half]
            c0 = (pair % half) * LANES
            xs = xr[:, c0:c0 + LANES]
            xs_half = (xs * lo_sel, xs * hi_sel)
            s_pair = s_ref[pair]
            y = jnp.zeros((q, LANES), F32)
            upd = jnp.zeros((SSD_STATE, LANES), F32)
            hs = (lane0 + 2 * pair, lane0 + 2 * pair + 1)
            e_in = []
            for hh, xh in zip(hs, xs_half):
                c_col = jnp.broadcast_to(cum[:, hh:hh + 1], (q, LANES))
                e_in.append(jnp.exp(c_col))
                decay = jnp.exp(jnp.where(mask, c_col - cum_t[hh:hh + 1, :], NEG_BIG))
                wts = (scores * decay * dt_t[hh:hh + 1, :]).astype(BF16)
                y = y + _dot(wts, xh)
                upd = upd + _dot((k_t * wout_t[hh:hh + 1, :]).astype(BF16), xh)
            y = y + jnp.where(lo_half, e_in[0], e_in[1]) * _dot(qm, s_pair.astype(BF16))
            e_tot_pair = jnp.where(lo_half, e_tot[:, hs[0]:hs[0] + 1], e_tot[:, hs[1]:hs[1] + 1])
            s_ref[pair] = s_pair * e_tot_pair + upd
            o_ref = out_refs[pair // half]
            if final:
                y = y + yf_refs[pair // half][:, c0:c0 + LANES].astype(F32)
                y = y + dsk_ref[:, pair * LANES:(pair + 1) * LANES] * xs.astype(F32)
                y = y * _silu(z_refs[pair // half][:, c0:c0 + LANES].astype(F32))
            o_ref[:, c0:c0 + LANES] = y.astype(o_ref.dtype)


def _ssd_scan(lay, xbc, zproj, dtproj, par, dskip, d_inner, reverse, yf):
    q = LANES
    m = xbc.shape[0]
    nsteps, row_block, _ = _scan_blocks(lay, q, reverse)
    hw = d_inner // 2
    assert hw == 2 * SSD_GROUPS * SSD_STATE
    n_pairs = d_inner // LANES
    final = yf is not None
    tok = lambda cb: (lambda b, s: (row_block(b, s), cb))
    in_specs = [pl.BlockSpec((q, hw), tok(0)), pl.BlockSpec((q, hw), tok(1)),
                pl.BlockSpec((q, hw), tok(2)),
                pl.BlockSpec((q, LANES), tok(0)),
                pl.BlockSpec((8, LANES), lambda b, s: (0, 0))]
    args = [xbc, xbc, xbc, dtproj, par]
    if final:
        in_specs += [pl.BlockSpec((q, hw), tok(0)), pl.BlockSpec((q, hw), tok(0)),
                     pl.BlockSpec((q, hw), tok(0)), pl.BlockSpec((q, hw), tok(1)),
                     pl.BlockSpec((1, d_inner), lambda b, s: (0, 0))]
        args += [yf[0], yf[1], zproj, zproj, dskip]
    kern = functools.partial(_ssd_scan_kernel, q=q, reverse=reverse, n_pairs=n_pairs, final=final)
    return pl.pallas_call(
        kern,
        grid=(lay.batch, nsteps),
        in_specs=in_specs,
        out_specs=[pl.BlockSpec((q, hw), tok(0)), pl.BlockSpec((q, hw), tok(0))],
        out_shape=[jax.ShapeDtypeStruct((m, hw), BF16), jax.ShapeDtypeStruct((m, hw), BF16)],
        scratch_shapes=[pltpu.VMEM((n_pairs, SSD_STATE, LANES), F32)],
        compiler_params=_cparams("parallel", "arbitrary"),
        name="ssd_scan_bwd" if reverse else "ssd_scan_fwd",
    )(*args)


def _ret_scan_kernel(*refs, q, reverse, hk, hv, final):
    it = iter(refs)
    q_ref, k_ref, v_ref, cos_ref, sin_ref, dl_ref = (next(it) for _ in range(6))
    yf_ref = g_ref = gng_ref = gnb_ref = None
    if final:
        yf_ref, g_ref, gng_ref, gnb_ref = (next(it) for _ in range(4))
    o_ref = next(it)
    s_ref, dec_ref, ein_ref, eout_ref = (next(it) for _ in range(4))
    half = hk // 2
    row = 1 if reverse else 0

    @pl.when(pl.program_id(1) == 0)
    def _():
        s_ref[...] = jnp.zeros_like(s_ref)

    @pl.when(pl.program_id(1) == 0)
    def _():
        ii = lax.broadcasted_iota(jnp.int32, (q, q), 0)
        jj = lax.broadcasted_iota(jnp.int32, (q, q), 1)
        dist = ((jj - ii) if reverse else (ii - jj)).astype(F32)
        mask = _tri_mask(q, reverse)
        pos = lax.broadcasted_iota(jnp.int32, (q, LANES), 0).astype(F32)
        for h in range(RET_HEADS):
            lg = _log_sigmoid(dl_ref[row:row + 1, h * LANES:(h + 1) * LANES])
            dec_ref[h] = jnp.exp(jnp.where(mask, dist * lg[:, :q], NEG_BIG))
            if reverse:
                ein_ref[h] = jnp.exp((q - pos) * lg)
                eout_ref[h] = jnp.exp(pos * lg)
            else:
                ein_ref[h] = jnp.exp((pos + 1.0) * lg)
                eout_ref[h] = jnp.exp((q - 1.0 - pos) * lg)

    cos = cos_ref[...]
    sin = sin_ref[...]

    def rot(ref, h):
        x0 = ref[:, h * hk:h * hk + half].astype(F32)
        x1 = ref[:, h * hk + half:(h + 1) * hk].astype(F32)
        return x0 * cos - x1 * sin, x0 * sin + x1 * cos

    scale = hk ** -0.5
    for h in range(RET_HEADS):
        q0, q1 = rot(q_ref, h)
        k0, k1 = rot(k_ref, h)
        k0, k1 = k0 * scale, k1 * scale
        qr = jnp.concatenate([q0, q1], axis=1).astype(BF16)
        kr = jnp.concatenate([k0, k1], axis=1).astype(BF16)
        v = v_ref[:, h * hv:(h + 1) * hv]
        scores = _dot_nt(qr, kr)
        y = _dot((scores * dec_ref[h]).astype(BF16), v)
        st = s_ref[h]
        ein = ein_ref[h]
        inter = _dot(qr, st.astype(BF16))
        y = y + jnp.concatenate([ein] * (hv // LANES), axis=1) * inter
        eout = eout_ref[h]
        ks = jnp.concatenate([k0 * eout, k1 * eout], axis=1).astype(BF16)
        e_tot = ein[0:1, :] * eout[0:1, :]
        s_ref[h] = st * jnp.concatenate([e_tot] * (hv // LANES), axis=1) + _dot_tn(ks, v)
        if final:
            y = y + yf_ref[:, h * hv:(h + 1) * hv].astype(F32)
            mu = jnp.mean(y, axis=1, keepdims=True)
            yc = y - mu
            var = jnp.mean(yc * yc, axis=1, keepdims=True)
            yn = yc * lax.rsqrt(var + EPS) * gng_ref[:, h * hv:(h + 1) * hv] + gnb_ref[:, h * hv:(h + 1) * hv]
            y = _silu(g_ref[:, h * hv:(h + 1) * hv].astype(F32)) * yn
        o_ref[:, h * hv:(h + 1) * hv] = y.astype(o_ref.dtype)


def _ret_scan(lay, qk, vg, cos, sin, dl, gn_g, gn_b, qk_dim, v_dim, reverse, yf):
    q = LANES
    m = qk.shape[0]
    nsteps, row_block, pos_block = _scan_blocks(lay, q, reverse)
    final = yf is not None
    hk, hv = qk_dim // RET_HEADS, v_dim // RET_HEADS
    tok = lambda w, cb: pl.BlockSpec((q, w), lambda b, s: (row_block(b, s), cb))
    pos = pl.BlockSpec((q, hk // 2), lambda b, s: (pos_block(b, s), 0))
    const = lambda shape: pl.BlockSpec(shape, lambda b, s: (0, 0))
    in_specs = [tok(qk_dim, 0), tok(qk_dim, 1), tok(v_dim, 0), pos, pos, const(dl.shape)]
    args = [qk, qk, vg, cos, sin, dl]
    if final:
        in_specs += [tok(v_dim, 0), tok(v_dim, 1), const((1, v_dim)), const((1, v_dim))]
        args += [yf, vg, gn_g.reshape(1, v_dim), gn_b.reshape(1, v_dim)]
    kern = functools.partial(_ret_scan_kernel, q=q, reverse=reverse, hk=hk, hv=hv, final=final)
    return pl.pallas_call(
        kern,
        grid=(lay.batch, nsteps),
        in_specs=in_specs,
        out_specs=tok(v_dim, 0),
        out_shape=jax.ShapeDtypeStruct((m, v_dim), BF16),
        scratch_shapes=[pltpu.VMEM((RET_HEADS, hk, hv), F32), pltpu.VMEM((RET_HEADS, q, q), F32),
                        pltpu.VMEM((RET_HEADS, q, LANES), F32), pltpu.VMEM((RET_HEADS, q, LANES), F32)],
        compiler_params=_cparams("parallel", "arbitrary"),
        name="ret_scan_bwd" if reverse else "ret_scan_fwd",
    )(*args)


def _vscan_kernel(*refs, kind, q, reverse, heads, hk, hv, final):
    it = iter(refs)
    if kind == "gla":
        q_ref, k_ref, v_ref, a_ref, wup_ref, ab_ref = (next(it) for _ in range(6))
    else:
        q_ref, v_ref, f_ref, lb_ref = (next(it) for _ in range(4))
    yf_ref = gate_ref = ng_ref = None
    if final:
        yf_ref, gate_ref, ng_ref = (next(it) for _ in range(3))
    o_ref = next(it)
    st_ref = next(it)

    @pl.when(pl.program_id(1) == 0)
    def _():
        st_ref[...] = jnp.zeros_like(st_ref)

    if kind == "gla":
        logit = _dot(a_ref[...], wup_ref[...]) + ab_ref[...]
        la = _log_sigmoid(logit) * (1.0 / GLA_TAU)
        qf = q_ref[...].astype(F32) * (hk ** -0.5)
        kf = k_ref[...].astype(F32)
    else:
        lb = lb_ref[...]
        f = lb + (1.0 - lb) * _sigmoid(f_ref[...].astype(F32))
        la = jnp.log(f)
        kf = 1.0 - f
        qf = _silu(q_ref[...].astype(F32))

    cum = _cumsum_rows(la)
    total = cum[q - 1:q, :]
    if reverse:
        cum = total - cum + la
        ref = cum[q // 2 - 1:q // 2, :]
    else:
        ref = cum[q // 2:q // 2 + 1, :]
    q_rel = qf * jnp.exp(cum - ref)
    k_rel = kf * jnp.exp(ref - cum)
    q_abs = (q_rel * jnp.exp(ref)).astype(BF16)
    k_out = (k_rel * jnp.exp(total - ref)).astype(BF16)
    q_rel = q_rel.astype(BF16)
    k_rel = k_rel.astype(BF16)
    e_tot = jnp.exp(total)
    mask = _tri_mask(q, reverse)

    for h in range(heads):
        ks = slice(h * hk, (h + 1) * hk)
        vs = slice(h * hv, (h + 1) * hv)
        v = v_ref[:, vs]
        scores = jnp.where(mask, _dot_nt(q_rel[:, ks], k_rel[:, ks]), 0.0)
        st = st_ref[h]
        y = _dot(scores.astype(BF16), v) + _dot_nt(q_abs[:, ks], st.astype(BF16))
        st_ref[h] = st * e_tot[:, ks] + _dot_tn(v, k_out[:, ks])
        if final:
            y = y + yf_ref[:, vs].astype(F32)
            ms = jnp.mean(y * y, axis=1, keepdims=True)
            y = y * lax.rsqrt(ms + EPS) * ng_ref[...] * _silu(gate_ref[:, vs].astype(F32))
        o_ref[:, vs] = y.astype(o_ref.dtype)


def _vscan(lay, kind, proj, extra, norm_g, heads, hk, hv, reverse, yf):
    q = 64
    m = proj.shape[0]
    nsteps, row_block, _ = _scan_blocks(lay, q, reverse)
    final = yf is not None
    kd, vd = heads * hk, heads * hv
    tok = lambda w, cb: pl.BlockSpec((q, w), lambda b, s: (row_block(b, s), cb))
    const = lambda shape, cb: pl.BlockSpec(shape, lambda b, s: (0, cb))
    d = 1 if reverse else 0
    if kind == "gla":
        a_low, wup, ab = extra
        in_specs = [tok(kd, 0), tok(kd, 1), tok(vd, 2 * kd // vd), tok(LANES, 0),
                    const((LANES, kd), d), const((1, kd), d)]
        args = [proj, proj, proj, a_low, wup, ab]
        gate_cb = 2 * kd // vd + 1
    else:
        (lb,) = extra
        in_specs = [tok(kd, 0), tok(vd, kd // vd), tok(kd, (kd + 2 * vd) // kd + d), const((1, kd), 0)]
        args = [proj, proj, proj, lb]
        gate_cb = kd // vd + 1
    if final:
        in_specs += [tok(vd, 0), tok(vd, gate_cb), const((1, hv), 0)]
        args += [yf, proj, norm_g.reshape(1, hv)]
    kern = functools.partial(_vscan_kernel, kind=kind, q=q, reverse=reverse, heads=heads, hk=hk, hv=hv,
                             final=final)
    return pl.pallas_call(
        kern,
        grid=(lay.batch, nsteps),
        in_specs=in_specs,
        out_specs=tok(vd, 0),
        out_shape=jax.ShapeDtypeStruct((m, vd), BF16),
        scratch_shapes=[pltpu.VMEM((heads, hv, hk), F32)],
        compiler_params=_cparams("parallel", "arbitrary"),
        name=f"{kind}_scan_{'bwd' if reverse else 'fwd'}",
    )(*args)


def _lower_bound_kernel(x_ref, o_ref):
    x = x_ref[...]
    mx = jnp.max(x, axis=0, keepdims=True)
    e = jnp.exp(x - mx)
    p = e / jnp.sum(e, axis=0, keepdims=True)
    acc = jnp.zeros_like(p[0:1])
    for i in range(x.shape[0]):
        acc = acc + p[i:i + 1]
        o_ref[i:i + 1, :] = acc - p[0:1]


def _lower_bounds(logits):
    return pl.pallas_call(_lower_bound_kernel, out_shape=jax.ShapeDtypeStruct(logits.shape, F32),
                          name="hgrn_lower_bounds")(logits)


def _conv_table(taps, bias, rows):
    c = bias.shape[0]
    tab = jnp.zeros((16, c), F32)
    tab = tab.at[jnp.asarray(rows)].set(taps).at[9].set(bias)
    return tab


def _ssd_mixer(lay, u, w_in, conv_w, conv_b, dt_bias, a_log, d_skip, d_model):
    d_inner = 2 * d_model
    heads = d_inner // SSD_HEAD_DIM
    conv_ch = d_inner + 2 * SSD_GROUPS * SSD_STATE
    cw = _conv_table(conv_w, conv_b, (3, 4, 5))
    xbc = _proj_conv(lay, u, w_in, None, cw, col0=d_inner, n=conv_ch, tn=1024, hr=2 * SUBLANES,
                     vertical=False, lat_period=lay.seq, name="ssd_in_proj_conv")
    zproj = _matmul(lay, u, w_in, 0, d_inner)
    dtproj = _matmul(lay, u, w_in, d_inner + conv_ch, 2 * heads, out_dtype=F32)
    par = jnp.concatenate([dt_bias.reshape(1, 2 * heads), -jnp.exp(a_log.astype(F32)).reshape(1, 2 * heads),
                           jnp.zeros((6, 2 * heads), F32)], axis=0)
    dskip = jnp.repeat(d_skip, SSD_HEAD_DIM).reshape(1, d_inner)
    yf = _ssd_scan(lay, xbc, zproj, dtproj, par, dskip, d_inner, False, None)
    ya, yb = _ssd_scan(lay, xbc, zproj, dtproj, par, dskip, d_inner, True, yf)
    return jnp.concatenate([ya, yb], axis=1)


def _deinterleave_heads(w, heads):
    d, n = w.shape
    hk = n // heads
    return w.reshape(d, heads, hk // 2, 2).transpose(0, 1, 3, 2).reshape(d, n)


def _ret_mixer(lay, u, w_in, decay_logit, gn_g, gn_b, d_model):
    qk, vd = d_model, 2 * d_model
    hk = qk // RET_HEADS
    w_qk = _deinterleave_heads(w_in[:, :2 * qk], 2 * RET_HEADS).astype(BF16)
    qkp = _matmul(lay, u, w_qk, 0, 2 * qk)
    vgp = _matmul(lay, u, w_in, 2 * qk, 2 * vd)
    half = hk // 2
    inv_freq = 1.0 / (10000.0 ** jnp.linspace(0.0, 1.0, half, dtype=F32))
    ang = jnp.arange(lay.ctx + lay.seq, dtype=F32)[:, None] * inv_freq[None, :]
    cos, sin = jnp.cos(ang), jnp.sin(ang)
    dl = jnp.repeat(decay_logit.astype(F32), LANES, axis=1)
    dl = jnp.concatenate([dl, jnp.zeros((6, dl.shape[1]), F32)], axis=0)
    yf = _ret_scan(lay, qkp, vgp, cos, sin, dl, gn_g, gn_b, qk, vd, False, None)
    return _ret_scan(lay, qkp, vgp, cos, sin, dl, gn_g, gn_b, qk, vd, True, yf)


def _gla_mixer(lay, u, w_in, w_alpha_up, alpha_b, norm_g, d_model):
    kd, vd = d_model // 2, d_model
    n_main = 2 * kd + 2 * vd
    proj = _matmul(lay, u, w_in, 0, n_main)
    w_low = jnp.pad(w_in[:, n_main:], ((0, 0), (0, LANES - 2 * GLA_RANK))).astype(BF16)
    a_low = _matmul(lay, u, w_low, 0, LANES)
    wup = jnp.zeros((LANES, 2 * kd), F32)
    wup = wup.at[0:GLA_RANK, 0:kd].set(w_alpha_up[0]).at[GLA_RANK:2 * GLA_RANK, kd:].set(w_alpha_up[1])
    ab = alpha_b.reshape(1, 2 * kd)
    args = (lay, "gla", proj, (a_low, wup.astype(BF16), ab), norm_g, GLA_HEADS, kd // GLA_HEADS,
            vd // GLA_HEADS)
    yf = _vscan(*args, False, None)
    return _vscan(*args, True, yf)


def _hgrn_mixer(lay, u, w_in, lower_bound, norm_g, d_model):
    heads = d_model // HGRN_HEAD_K
    proj = _matmul(lay, u, w_in, 0, w_in.shape[1])
    args = (lay, "hgrn", proj, (lower_bound.reshape(1, d_model),), norm_g, heads, HGRN_HEAD_K,
            d_model // heads)
    yf = _vscan(*args, False, None)
    return _vscan(*args, True, yf)


def kernel(x, c, ctx, c_ctx, mod_w, mod_b, ln_mix_g, ln_mix_b, ln_ffn_g, ln_ffn_b, ffn_w_up, ffn_conv_w, ffn_conv_b, ffn_w_down, hgrn_lb_logits, ssd_w_in, ssd_conv_w, ssd_conv_b, ssd_dt_bias, ssd_a_log, ssd_d, ssd_norm_g, ssd_w_out, ret_w_in, ret_decay_logit, ret_gn_g, ret_gn_b, ret_w_out, gla_w_in, gla_w_alpha_up, gla_alpha_b, gla_norm_g, gla_w_out, hgrn_w_in, hgrn_norm_g, hgrn_w_out):
    batch, seq, d = x.shape
    ctx_len = ctx.shape[1]
    depth = mod_w.shape[0]
    hidden = ffn_w_down.shape[1]
    lay = _Layout(batch, seq, ctx_len)
    alpha = (2.0 * depth) ** 0.25
    n_mixers = 4
    assert GRID_W & (GRID_W - 1) == 0 and ctx_len & (ctx_len - 1) == 0 and seq & (seq - 1) == 0

    c_rows = jnp.concatenate([c, c_ctx[None], jnp.zeros((SUBLANES - batch - 1, d), F32)], axis=0)
    mods = _modulation(c_rows, mod_w, mod_b).reshape(depth, SUBLANES, 1, 6 * d)
    SH_M, SC_M, G_M, SH_F, SC_F, G_F = range(6)

    lower_bounds = _lower_bounds(hgrn_lb_logits.astype(F32))

    h, u = _modulate(lay, x.reshape(lay.m_lat, d), ctx.reshape(lay.m_ctx, d), mods, 0, SC_M, SH_M)

    ffn_tn = 512
    hp = -(-hidden // ffn_tn) * ffn_tn
    tk_ffn = hp // 4 if (hp // 4) % LANES == 0 else ffn_tn
    for i in range(depth):
        kind, j = i % n_mixers, i // n_mixers
        last = i == depth - 1
        m_rows = lay.m_lat if last else None
        rms_gain = None
        if kind == 0:
            y = _ssd_mixer(lay, u, ssd_w_in[j], ssd_conv_w[j], ssd_conv_b[j], ssd_dt_bias[j], ssd_a_log[j],
                           ssd_d[j], d)
            w_out, rms_gain = ssd_w_out[j], ssd_norm_g[j]
        elif kind == 1:
            y = _ret_mixer(lay, u, ret_w_in[j], ret_decay_logit[j], ret_gn_g[j], ret_gn_b[j], d)
            w_out = ret_w_out[j]
        elif kind == 2:
            y = _gla_mixer(lay, u, gla_w_in[j], gla_w_alpha_up[j], gla_alpha_b[j], gla_norm_g[j], d)
            w_out = gla_w_out[j]
        else:
            y = _hgrn_mixer(lay, u, hgrn_w_in[j], lower_bounds[i], hgrn_norm_g[j], d)
            w_out = hgrn_w_out[j]
        kd = w_out.shape[0]
        h, u = _matmul_ln(lay, y, w_out.astype(BF16), h, mods, i, G_M, ln_mix_g[i], ln_mix_b[i], alpha,
                          next_mod=(i, SC_F, SH_F), rms_gain=rms_gain, tk=min(kd, 2048), m_rows=m_rows)

        pad = hp - hidden
        wg = jnp.pad(ffn_w_up[i][:, :hidden], ((0, 0), (0, pad))).astype(BF16)
        wu = jnp.pad(ffn_w_up[i][:, hidden:], ((0, 0), (0, pad))).astype(BF16)
        cw = jnp.pad(_conv_table(ffn_conv_w[i].reshape(9, hidden), ffn_conv_b[i], tuple(range(9))),
                     ((0, 0), (0, pad)))
        wd = jnp.pad(ffn_w_down[i], ((0, pad), (0, 0))).astype(BF16)
        act = _proj_conv(lay, u, wg, wu, cw, col0=0, n=hp, tn=ffn_tn, hr=GRID_W, vertical=True,
                         lat_period=GRID_W, m_rows=m_rows, name="ffn_up_conv")
        nxt = (i + 1, SC_M, SH_M) if not last else None
        h, u = _matmul_ln(lay, act, wd, h, mods, i, G_F, ln_ffn_g[i], ln_ffn_b[i], alpha,
                          next_mod=nxt, tk=tk_ffn, m_rows=m_rows)
    return h.reshape(batch, seq, d)
```

```python
import functools

import jax
import jax.numpy as jnp
from jax import lax
from jax.experimental import pallas as pl
from jax.experimental.pallas import tpu as pltpu

F32 = jnp.float32
BF16 = jnp.bfloat16

EPS = 1e-5
GRID_W = 64
SSD_HEAD_DIM = 64
SSD_GROUPS = 8
SSD_STATE = 128
RET_HEADS = 8
GLA_HEADS = 4
GLA_RANK = 16
GLA_TAU = 16.0
HGRN_HEAD_K = 128
LANES = 128
SUBLANES = 8
VMEM_LIMIT = 56 * 1024 * 1024
NEG_BIG = -1e30


def _cparams(*sem):
    return pltpu.CompilerParams(dimension_semantics=sem, vmem_limit_bytes=VMEM_LIMIT)


def _sigmoid(x):
    return 1.0 / (1.0 + jnp.exp(-x))


def _silu(x):
    return x * _sigmoid(x)


def _softplus(x):
    return jnp.maximum(x, 0.0) + jnp.log(1.0 + jnp.exp(-jnp.abs(x)))


def _log_sigmoid(x):
    return jnp.minimum(x, 0.0) - jnp.log(1.0 + jnp.exp(-jnp.abs(x)))


def _pow2_tile(limit, *sizes):
    t = 1
    while t * 2 <= limit and all(s % (t * 2) == 0 for s in sizes):
        t *= 2
    return t


def _dot(a, b):
    return jnp.dot(a, b, preferred_element_type=F32)


def _dot_nt(a, b):
    return lax.dot_general(a, b, (((1,), (1,)), ((), ())), preferred_element_type=F32)


def _dot_tn(a, b):
    return lax.dot_general(a, b, (((0,), (0,)), ((), ())), preferred_element_type=F32)


class _Layout:
    def __init__(self, batch, seq, ctx):
        self.batch, self.seq, self.ctx = batch, seq, ctx
        self.m_lat = batch * seq
        self.m_ctx = batch * ctx
        self.m = self.m_lat + self.m_ctx
        self.tm = _pow2_tile(1024, seq, self.m_ctx)
        self.tm_ln = _pow2_tile(512, seq, self.m_ctx)

    def mod_row(self, tile, tm):
        start = tile * tm
        return jnp.where(start < self.m_lat, start // self.seq, self.batch)


def _mod_kernel(c_ref, w_ref, b_ref, o_ref):
    c = _silu(c_ref[...]).astype(BF16)
    o_ref[...] = _dot(c, w_ref[...].astype(BF16)) + b_ref[...]


def _modulation(c_rows, mod_w, mod_b):
    depth, d, n = mod_w.shape
    rows = c_rows.shape[0]
    tn = _pow2_tile(1024, n)
    return pl.pallas_call(
        _mod_kernel,
        grid=(depth, n // tn),
        in_specs=[pl.BlockSpec((rows, d), lambda l, j: (0, 0)),
                  pl.BlockSpec((None, d, tn), lambda l, j: (l, 0, j)),
                  pl.BlockSpec((None, 1, tn), lambda l, j: (l, 0, j))],
        out_specs=pl.BlockSpec((None, rows, tn), lambda l, j: (l, 0, j)),
        out_shape=jax.ShapeDtypeStruct((depth, rows, n), F32),
        compiler_params=_cparams("parallel", "parallel"),
        name="modulation",
    )(c_rows, mod_w, mod_b.reshape(depth, 1, n))


def _modulate_kernel(x_ref, c_ref, sc_ref, sh_ref, h_ref, u_ref, *, n_lat_tiles):
    def emit(src):
        h = src[...]
        h_ref[...] = h
        u_ref[...] = (h * (1.0 + sc_ref[...]) + sh_ref[...]).astype(BF16)

    pl.when(pl.program_id(0) < n_lat_tiles)(lambda: emit(x_ref))
    pl.when(pl.program_id(0) >= n_lat_tiles)(lambda: emit(c_ref))


def _modulate(lay, x2, c2, mods, layer, sc_blk, sh_blk):
    d = x2.shape[1]
    tm = lay.tm_ln
    nl = lay.m_lat // tm
    nc = lay.m_ctx // tm
    mod_spec = lambda blk: pl.BlockSpec((None, None, 1, d), lambda i: (layer, lay.mod_row(i, tm), 0, blk))
    tok = pl.BlockSpec((tm, d), lambda i: (i, 0))
    return pl.pallas_call(
        functools.partial(_modulate_kernel, n_lat_tiles=nl),
        grid=(nl + nc,),
        in_specs=[pl.BlockSpec((tm, d), lambda i: (jnp.minimum(i, nl - 1), 0)),
                  pl.BlockSpec((tm, d), lambda i: (jnp.maximum(i - nl, 0), 0)),
                  mod_spec(sc_blk), mod_spec(sh_blk)],
        out_specs=[tok, tok],
        out_shape=[jax.ShapeDtypeStruct((lay.m, d), F32), jax.ShapeDtypeStruct((lay.m, d), BF16)],
        compiler_params=_cparams("parallel"),
        name="modulate",
    )(x2, c2, mods, mods)


def _mm_kernel(x_ref, w_ref, o_ref, *scratch, cast_w):
    if cast_w:
        wb_ref, = scratch

        @pl.when(pl.program_id(1) == 0)
        def _():
            wb_ref[...] = w_ref[...].astype(BF16)

        w = wb_ref[...]
    else:
        w = w_ref[...]
    o_ref[...] = _dot(x_ref[...], w).astype(o_ref.dtype)


def _matmul(lay, x, w, col0, n, out_dtype=BF16):
    m, k = x.shape
    tm = lay.tm
    tn = _pow2_tile(1024, n, col0) if col0 else _pow2_tile(1024, n)
    cast_w = w.dtype != BF16
    cb0 = col0 // tn
    return pl.pallas_call(
        functools.partial(_mm_kernel, cast_w=cast_w),
        grid=(n // tn, m // tm),
        in_specs=[pl.BlockSpec((tm, k), lambda j, i: (i, 0)),
                  pl.BlockSpec((k, tn), lambda j, i: (0, cb0 + j))],
        out_specs=pl.BlockSpec((tm, tn), lambda j, i: (i, j)),
        out_shape=jax.ShapeDtypeStruct((m, n), out_dtype),
        scratch_shapes=[pltpu.VMEM((k, tn), BF16)] if cast_w else [],
        compiler_params=_cparams("parallel", "arbitrary"),
        name="in_proj",
    )(x, w)


def _mm_ln_kernel(*refs, nk, n_tiles, kdim, alpha, rms, emit_u):
    it = iter(refs)
    x_ref, w_ref, h_ref, gate_ref, lng_ref, lnb_ref = (next(it) for _ in range(6))
    sc_ref = sh_ref = rg_ref = u_ref = ssq_ref = None
    if emit_u:
        sc_ref, sh_ref = next(it), next(it)
    if rms:
        rg_ref = next(it)
    hout_ref = next(it)
    if emit_u:
        u_ref = next(it)
    acc_refs = (next(it), next(it))
    ssq_refs = (next(it), next(it)) if rms else (None, None)
    i = pl.program_id(0)
    k = pl.program_id(1)

    def product():
        x = x_ref[...]
        ssq = None
        if rms:
            xf = x.astype(F32)
            ssq = jnp.sum(xf * xf, axis=1, keepdims=True)
            x = (xf * rg_ref[...]).astype(BF16)
        return _dot(x, w_ref[...]), ssq

    @pl.when(jnp.logical_and(i == 0, k == 0))
    def _():
        acc_refs[1][...] = jnp.zeros_like(acc_refs[1])
        if rms:
            ssq_refs[1][...] = jnp.zeros_like(ssq_refs[1])

    for slot in (0, 1):
        acc_ref, ssq_ref = acc_refs[slot], ssq_refs[slot]
        prev_acc, prev_ssq = acc_refs[1 - slot], ssq_refs[1 - slot]

        @pl.when(jnp.logical_and(k == 0, i % 2 == slot))
        def _(acc_ref=acc_ref, ssq_ref=ssq_ref, prev_acc=prev_acc, prev_ssq=prev_ssq):
            o_new, ssq_new = product()
            acc_ref[...] = o_new
            if rms:
                ssq_ref[...] = ssq_new
            o = prev_acc[...]
            if rms:
                o = o * lax.rsqrt(prev_ssq[...] * (1.0 / kdim) + EPS)
            y = alpha * h_ref[...] + gate_ref[...] * o
            mu = jnp.mean(y, axis=1, keepdims=True)
            yc = y - mu
            var = jnp.mean(yc * yc, axis=1, keepdims=True)
            hn = yc * lax.rsqrt(var + EPS) * lng_ref[...] + lnb_ref[...]
            hout_ref[...] = hn
            if emit_u:
                u_ref[...] = (hn * (1.0 + sc_ref[...]) + sh_ref[...]).astype(BF16)

        if nk > 1:
            @pl.when(jnp.logical_and(jnp.logical_and(k > 0, i < n_tiles), i % 2 == slot))
            def _(acc_ref=acc_ref, ssq_ref=ssq_ref):
                o_new, ssq_new = product()
                acc_ref[...] += o_new
                if rms:
                    ssq_ref[...] += ssq_new


def _matmul_ln(lay, x, w, h, mods, gate_layer, gate_blk, ln_g, ln_b, alpha, *,
               next_mod=None, rms_gain=None, tk, m_rows=None, w_layer=None):
    kdim = x.shape[1]
    m = x.shape[0] if m_rows is None else m_rows
    d = w.shape[-1]
    tm = lay.tm_ln
    nk = kdim // tk
    n_tiles = m // tm
    emit_u = next_mod is not None
    rms = rms_gain is not None
    cur = lambda i: jnp.minimum(i, n_tiles - 1)
    prev = lambda i: jnp.maximum(i - 1, 0)

    def mod_spec(layer, blk):
        return pl.BlockSpec((None, None, 1, d), lambda i, k: (layer, lay.mod_row(prev(i), tm), 0, blk))

    row_spec = pl.BlockSpec((1, d), lambda i, k: (0, 0))
    w_spec = (pl.BlockSpec((tk, d), lambda i, k: (k, 0)) if w_layer is None else
              pl.BlockSpec((None, tk, d), lambda i, k: (w_layer, k, 0)))
    in_specs = [pl.BlockSpec((tm, tk), lambda i, k: (cur(i), k)),
                w_spec,
                pl.BlockSpec((tm, d), lambda i, k: (prev(i), 0)),
                mod_spec(gate_layer, gate_blk), row_spec, row_spec]
    args = [x, w, h, mods, ln_g.reshape(1, d), ln_b.reshape(1, d)]
    if emit_u:
        nl, sc_blk, sh_blk = next_mod
        in_specs += [mod_spec(nl, sc_blk), mod_spec(nl, sh_blk)]
        args += [mods, mods]
    if rms:
        in_specs.append(pl.BlockSpec((1, tk), lambda i, k: (0, k)))
        args.append(rms_gain.reshape(1, kdim))
    out_specs = [pl.BlockSpec((tm, d), lambda i, k: (prev(i), 0))]
    out_shape = [jax.ShapeDtypeStruct((m, d), F32)]
    if emit_u:
        out_specs.append(pl.BlockSpec((tm, d), lambda i, k: (prev(i), 0)))
        out_shape.append(jax.ShapeDtypeStruct((m, d), BF16))
    scratch = [pltpu.VMEM((tm, d), F32), pltpu.VMEM((tm, d), F32)]
    if rms:
        scratch += [pltpu.VMEM((tm, 1), F32), pltpu.VMEM((tm, 1), F32)]
    outs = pl.pallas_call(
        functools.partial(_mm_ln_kernel, nk=nk, n_tiles=n_tiles, kdim=kdim, alpha=alpha, rms=rms,
                          emit_u=emit_u),
        grid=(n_tiles + 1, nk),
        in_specs=in_specs, out_specs=out_specs, out_shape=out_shape, scratch_shapes=scratch,
        compiler_params=_cparams("arbitrary", "arbitrary"),
        name="out_proj_ln",
    )(*args)
    return (outs[0], outs[1]) if emit_u else (outs[0], None)


_EXT_PAD = SUBLANES


def _proj_conv_kernel(*refs, tm, hr, vertical, second, rb, tiles_per_img, n_lat_tiles,
                      lat_period, ctx_len):
    it = iter(refs)
    xp_ref, x_ref, xn_ref, wg_ref = (next(it) for _ in range(4))
    wu_ref = next(it) if second else None
    cw_ref, o_ref = next(it), next(it)
    o2_ref = next(it) if second == "out" else None
    ext_ref = next(it)
    up_ref = next(it) if second == "mul" else None
    has_up = second == "mul"
    i = pl.program_id(1)
    wg = wg_ref[...]
    is_lat = i < n_lat_tiles
    period = jnp.where(is_lat, lat_period, ctx_len)
    tn = o_ref.shape[1]
    base = _EXT_PAD + hr
    vert = is_lat.astype(F32)
    tap_rows = (0, 1, 2) if vertical else (1,)

    prev = _dot(xp_ref[...], wg)
    nxt = _dot(xn_ref[...], wg)
    if vertical:
        t_img = i % tiles_per_img
        prev = prev * jnp.logical_and(is_lat, t_img > 0).astype(F32)
        nxt = nxt * jnp.logical_and(is_lat, t_img < tiles_per_img - 1).astype(F32)
    zpad = jnp.zeros((_EXT_PAD, tn), F32)
    ext_ref[0:_EXT_PAD, :] = zpad
    ext_ref[_EXT_PAD:base, :] = prev
    ext_ref[base:base + tm, :] = _dot(x_ref[...], wg)
    ext_ref[base + tm:base + tm + hr, :] = nxt
    ext_ref[base + tm + hr:base + tm + hr + _EXT_PAD, :] = zpad
    if has_up:
        up_ref[...] = _dot(x_ref[...], wu_ref[...])
    elif second == "out":
        o2_ref[...] = _dot(x_ref[...], wu_ref[...]).astype(o2_ref.dtype)

    cw = cw_ref[...]
    taps = {a: [cw[3 * a + b:3 * a + b + 1, :] * (1.0 if a == 1 else vert) for b in range(3)]
            for a in tap_rows}
    bias = cw[9:10, :]
    hb = _EXT_PAD

    for blk in range(tm // rb):
        r0 = blk * rb
        col = (lax.broadcasted_iota(jnp.int32, (rb, tn), 0) + (i * tm + r0)) & (period - 1)
        s_left = s_mid = s_right = None
        for a in tap_rows:
            lo = base + r0 + (a - 1) * hr - hb
            e = ext_ref[lo:lo + rb + 2 * hb, :]
            l, c, r = taps[a][0] * e, taps[a][1] * e[hb:hb + rb], taps[a][2] * e
            s_left = l if s_left is None else s_left + l
            s_mid = c if s_mid is None else s_mid + c
            s_right = r if s_right is None else s_right + r
        left = pltpu.roll(s_left, 1, 0)[hb:hb + rb]
        right = pltpu.roll(s_right, rb + 2 * hb - 1, 0)[hb:hb + rb]
        acc = (s_mid + bias) + jnp.where(col == 0, 0.0, left) + jnp.where(col == period - 1, 0.0, right)
        act = _silu(acc)
        if has_up:
            act = act * up_ref[r0:r0 + rb, :]
        o_ref[r0:r0 + rb, :] = act.astype(BF16)


def _proj_conv(lay, u, w, cw, *, layer=None, w_side=None, tn, hr, vertical, lat_period, m_rows=None,
               name):
    d = u.shape[1]
    m = u.shape[0] if m_rows is None else m_rows
    n = w.shape[-1]
    tm = lay.tm
    hpt = tm // hr
    last_halo = u.shape[0] // hr - 1
    second = "mul" if layer is not None else ("out" if w_side is not None else None)
    steps = n // tn
    tn2 = tn if second == "mul" else (w_side.shape[1] // steps if second == "out" else 0)
    kern = functools.partial(_proj_conv_kernel, tm=tm, hr=hr, vertical=vertical, second=second,
                             rb=min(tm, 128), tiles_per_img=lay.seq // tm,
                             n_lat_tiles=lay.m_lat // tm, lat_period=lat_period, ctx_len=lay.ctx)
    in_specs = [pl.BlockSpec((hr, d), lambda j, i: (jnp.maximum(i * hpt - 1, 0), 0)),
                pl.BlockSpec((tm, d), lambda j, i: (i, 0)),
                pl.BlockSpec((hr, d), lambda j, i: (jnp.minimum((i + 1) * hpt, last_halo), 0))]
    args = [u, u, u]
    tile = lambda width: pl.BlockSpec((tm, width), lambda j, i: (i, j))
    out_specs = [tile(tn)]
    out_shape = [jax.ShapeDtypeStruct((m, n), BF16)]
    scratch = [pltpu.VMEM((tm + 2 * hr + 2 * _EXT_PAD, tn), F32)]
    if second == "mul":
        in_specs += [pl.BlockSpec((None, None, d, tn), lambda j, i: (layer, 0, 0, j)),
                     pl.BlockSpec((None, None, d, tn), lambda j, i: (layer, 1, 0, j)),
                     pl.BlockSpec((None, 16, tn), lambda j, i: (layer, 0, j))]
        args += [w, w, cw]
        scratch.append(pltpu.VMEM((tm, tn), F32))
    else:
        in_specs.append(pl.BlockSpec((d, tn), lambda j, i: (0, j)))
        args.append(w)
        if second == "out":
            assert w_side.shape[1] == steps * tn2 and tn2 % LANES == 0
            in_specs.append(pl.BlockSpec((d, tn2), lambda j, i: (0, j)))
            args.append(w_side)
            out_specs.append(tile(tn2))
            out_shape.append(jax.ShapeDtypeStruct((m, w_side.shape[1]), BF16))
        in_specs.append(pl.BlockSpec((16, tn), lambda j, i: (0, j)))
        args.append(cw)
    outs = pl.pallas_call(
        kern,
        grid=(steps, m // tm),
        in_specs=in_specs, out_specs=out_specs, out_shape=out_shape, scratch_shapes=scratch,
        compiler_params=_cparams("parallel", "parallel"),
        name=name,
    )(*args)
    return outs if second == "out" else outs[0]


def _scan_blocks(lay, q, reverse):
    nctx, nlat = lay.ctx // q, lay.seq // q
    ctx0 = lay.m_lat // q

    def row_block(b, s):
        cj = (nctx - 1 - s) if reverse else s
        lj = (nlat - 1 - (s - nctx)) if reverse else (s - nctx)
        return jnp.where(s < nctx, ctx0 + b * nctx + cj, b * nlat + lj)

    def pos_block(b, s):
        if not reverse:
            return s
        return jnp.where(s < nctx, nctx - 1 - s, nctx + nlat - 1 - (s - nctx))

    return nctx + nlat, row_block, pos_block


def _cumsum_rows(x):
    n = x.shape[0]
    rows = lax.broadcasted_iota(jnp.int32, x.shape, 0)
    shift = 1
    while shift < n:
        x = x + jnp.where(rows >= shift, pltpu.roll(x, shift, 0), 0.0)
        shift *= 2
    return x


def _tri_mask(q, reverse):
    ii = lax.broadcasted_iota(jnp.int32, (q, q), 0)
    jj = lax.broadcasted_iota(jnp.int32, (q, q), 1)
    return (ii <= jj) if reverse else (ii >= jj)


def _ssd_scan_kernel(*refs, q, reverse, n_pairs, final):
    it = iter(refs)
    xs_refs = (next(it), next(it))
    bc_ref, dt_ref, par_ref = next(it), next(it), next(it)
    yf_ref = z_refs = dsk_ref = None
    if final:
        yf_ref = next(it)
        z_refs = (next(it), next(it))
        dsk_ref = next(it)
    o_ref = next(it)
    s_ref = next(it)

    @pl.when(pl.program_id(1) == 0)
    def _():
        s_ref[...] = jnp.zeros_like(s_ref)

    lane0 = 64 if reverse else 0
    dt = _softplus(dt_ref[...] + par_ref[0:1, :])
    la = dt * par_ref[1:2, :]
    cum = _cumsum_rows(la)
    total = cum[q - 1:q, :]
    if reverse:
        cum = total - cum + la
    cum_t = cum.T
    dt_t = dt.T
    wout_t = (jnp.exp(total - cum) * dt).T
    e_tot = jnp.exp(total)
    mask = _tri_mask(q, reverse)
    lane = lax.broadcasted_iota(jnp.int32, (1, LANES), 1)
    lo_half = lane < SSD_HEAD_DIM
    lo_sel = jnp.where(lo_half, 1.0, 0.0).astype(BF16)
    hi_sel = jnp.where(lo_half, 0.0, 1.0).astype(BF16)
    half = n_pairs // 2
    pairs_per_group = n_pairs // SSD_GROUPS

    for g in range(SSD_GROUPS):
        k = bc_ref[:, g * SSD_STATE:(g + 1) * SSD_STATE]
        qm = bc_ref[:, (SSD_GROUPS + g) * SSD_STATE:(SSD_GROUPS + g + 1) * SSD_STATE]
        scores = _dot_nt(qm, k)
        k_t = k.astype(F32).T
        for p in range(pairs_per_group):
            pair = g * pairs_per_group + p
            xr = xs_refs[pair // half]
            c0 = (pair % half) * LANES
            xs = xr[:, c0:c0 + LANES]
            xs_half = (xs * lo_sel, xs * hi_sel)
            s_pair = s_ref[pair]
            y = jnp.zeros((q, LANES), F32)
            upd = jnp.zeros((SSD_STATE, LANES), F32)
            hs = (lane0 + 2 * pair, lane0 + 2 * pair + 1)
            e_in = []
            for hh, xh in zip(hs, xs_half):
                c_col = jnp.broadcast_to(cum[:, hh:hh + 1], (q, LANES))
                e_in.append(jnp.exp(c_col))
                decay = jnp.exp(jnp.where(mask, c_col - cum_t[hh:hh + 1, :], NEG_BIG))
                wts = (scores * decay * dt_t[hh:hh + 1, :]).astype(BF16)
                y = y + _dot(wts, xh)
                upd = upd + _dot((k_t * wout_t[hh:hh + 1, :]).astype(BF16), xh)
            y = y + jnp.where(lo_half, e_in[0], e_in[1]) * _dot(qm, s_pair.astype(BF16))
            e_tot_pair = jnp.where(lo_half, e_tot[:, hs[0]:hs[0] + 1], e_tot[:, hs[1]:hs[1] + 1])
            s_ref[pair] = s_pair * e_tot_pair + upd
            oc = slice(pair * LANES, (pair + 1) * LANES)
            if final:
                y = y + yf_ref[:, oc].astype(F32)
                y = y + dsk_ref[:, oc] * xs.astype(F32)
                y = y * _silu(z_refs[pair // half][:, c0:c0 + LANES].astype(F32))
            o_ref[:, oc] = y.astype(o_ref.dtype)


def _ssd_scan(lay, xbc, zproj, dtproj, par, dskip, d_inner, reverse, yf):
    q = LANES
    m = xbc.shape[0]
    nsteps, row_block, _ = _scan_blocks(lay, q, reverse)
    hw = d_inner // 2
    assert hw == 2 * SSD_GROUPS * SSD_STATE
    n_pairs = d_inner // LANES
    final = yf is not None
    tok = lambda cb: (lambda b, s: (row_block(b, s), cb))
    in_specs = [pl.BlockSpec((q, hw), tok(0)), pl.BlockSpec((q, hw), tok(1)),
                pl.BlockSpec((q, hw), tok(2)),
                pl.BlockSpec((q, LANES), tok(0)),
                pl.BlockSpec((8, LANES), lambda b, s: (0, 0))]
    args = [xbc, xbc, xbc, dtproj, par]
    if final:
        in_specs += [pl.BlockSpec((q, d_inner), tok(0)),
                     pl.BlockSpec((q, hw), tok(0)), pl.BlockSpec((q, hw), tok(1)),
                     pl.BlockSpec((1, d_inner), lambda b, s: (0, 0))]
        args += [yf, zproj, zproj, dskip]
    kern = functools.partial(_ssd_scan_kernel, q=q, reverse=reverse, n_pairs=n_pairs, final=final)
    return pl.pallas_call(
        kern,
        grid=(lay.batch, nsteps),
        in_specs=in_specs,
        out_specs=pl.BlockSpec((q, d_inner), tok(0)),
        out_shape=jax.ShapeDtypeStruct((m, d_inner), BF16),
        scratch_shapes=[pltpu.VMEM((n_pairs, SSD_STATE, LANES), F32)],
        compiler_params=_cparams("parallel", "arbitrary"),
        name="ssd_scan_bwd" if reverse else "ssd_scan_fwd",
    )(*args)


def _ret_scan_kernel(*refs, q, reverse, hk, hv, final):
    it = iter(refs)
    q_ref, k_ref, v_ref, cos_ref, sin_ref, dl_ref = (next(it) for _ in range(6))
    yf_ref = g_ref = gng_ref = gnb_ref = None
    if final:
        yf_ref, g_ref, gng_ref, gnb_ref = (next(it) for _ in range(4))
    o_ref = next(it)
    s_ref, dec_ref, ein_ref, eout_ref = (next(it) for _ in range(4))
    half = hk // 2
    row = 1 if reverse else 0

    @pl.when(pl.program_id(1) == 0)
    def _():
        s_ref[...] = jnp.zeros_like(s_ref)

    @pl.when(pl.program_id(1) == 0)
    def _():
        ii = lax.broadcasted_iota(jnp.int32, (q, q), 0)
        jj = lax.broadcasted_iota(jnp.int32, (q, q), 1)
        dist = ((jj - ii) if reverse else (ii - jj)).astype(F32)
        mask = _tri_mask(q, reverse)
        pos = lax.broadcasted_iota(jnp.int32, (q, LANES), 0).astype(F32)
        for h in range(RET_HEADS):
            lg = _log_sigmoid(dl_ref[row:row + 1, h * LANES:(h + 1) * LANES])
            dec_ref[h] = jnp.exp(jnp.where(mask, dist * lg[:, :q], NEG_BIG))
            if reverse:
                ein_ref[h] = jnp.exp((q - pos) * lg)
                eout_ref[h] = jnp.exp(pos * lg)
            else:
                ein_ref[h] = jnp.exp((pos + 1.0) * lg)
                eout_ref[h] = jnp.exp((q - 1.0 - pos) * lg)

    cos = cos_ref[...]
    sin = sin_ref[...]

    def rot(ref, h):
        x0 = ref[:, h * hk:h * hk + half].astype(F32)
        x1 = ref[:, h * hk + half:(h + 1) * hk].astype(F32)
        return x0 * cos - x1 * sin, x0 * sin + x1 * cos

    scale = hk ** -0.5
    for h in range(RET_HEADS):
        q0, q1 = rot(q_ref, h)
        k0, k1 = rot(k_ref, h)
        k0, k1 = k0 * scale, k1 * scale
        qr = jnp.concatenate([q0, q1], axis=1).astype(BF16)
        kr = jnp.concatenate([k0, k1], axis=1).astype(BF16)
        v = v_ref[:, h * hv:(h + 1) * hv]
        scores = _dot_nt(qr, kr)
        y = _dot((scores * dec_ref[h]).astype(BF16), v)
        st = s_ref[h]
        ein = ein_ref[h]
        inter = _dot(qr, st.astype(BF16))
        y = y + jnp.concatenate([ein] * (hv // LANES), axis=1) * inter
        eout = eout_ref[h]
        ks = jnp.concatenate([k0 * eout, k1 * eout], axis=1).astype(BF16)
        e_tot = ein[0:1, :] * eout[0:1, :]
        s_ref[h] = st * jnp.concatenate([e_tot] * (hv // LANES), axis=1) + _dot_tn(ks, v)
        if final:
            y = y + yf_ref[:, h * hv:(h + 1) * hv].astype(F32)
            mu = jnp.mean(y, axis=1, keepdims=True)
            yc = y - mu
            var = jnp.mean(yc * yc, axis=1, keepdims=True)
            yn = yc * lax.rsqrt(var + EPS) * gng_ref[:, h * hv:(h + 1) * hv] + gnb_ref[:, h * hv:(h + 1) * hv]
            y = _silu(g_ref[:, h * hv:(h + 1) * hv].astype(F32)) * yn
        o_ref[:, h * hv:(h + 1) * hv] = y.astype(o_ref.dtype)


def _ret_scan(lay, qk, vg, cos, sin, dl, gn_g, gn_b, qk_dim, v_dim, reverse, yf):
    q = LANES
    m = qk.shape[0]
    nsteps, row_block, pos_block = _scan_blocks(lay, q, reverse)
    final = yf is not None
    hk, hv = qk_dim // RET_HEADS, v_dim // RET_HEADS
    tok = lambda w, cb: pl.BlockSpec((q, w), lambda b, s: (row_block(b, s), cb))
    pos = pl.BlockSpec((q, hk // 2), lambda b, s: (pos_block(b, s), 0))
    const = lambda shape: pl.BlockSpec(shape, lambda b, s: (0, 0))
    in_specs = [tok(qk_dim, 0), tok(qk_dim, 1), tok(v_dim, 0), pos, pos, const(dl.shape)]
    args = [qk, qk, vg, cos, sin, dl]
    if final:
        in_specs += [tok(v_dim, 0), tok(v_dim, 1), const((1, v_dim)), const((1, v_dim))]
        args += [yf, vg, gn_g.reshape(1, v_dim), gn_b.reshape(1, v_dim)]
    kern = functools.partial(_ret_scan_kernel, q=q, reverse=reverse, hk=hk, hv=hv, final=final)
    return pl.pallas_call(
        kern,
        grid=(lay.batch, nsteps),
        in_specs=in_specs,
        out_specs=tok(v_dim, 0),
        out_shape=jax.ShapeDtypeStruct((m, v_dim), BF16),
        scratch_shapes=[pltpu.VMEM((RET_HEADS, hk, hv), F32), pltpu.VMEM((RET_HEADS, q, q), F32),
                        pltpu.VMEM((RET_HEADS, q, LANES), F32), pltpu.VMEM((RET_HEADS, q, LANES), F32)],
        compiler_params=_cparams("parallel", "arbitrary"),
        name="ret_scan_bwd" if reverse else "ret_scan_fwd",
    )(*args)


def _vscan_kernel(*refs, kind, q, reverse, heads, hk, hv, final):
    it = iter(refs)
    if kind == "gla":
        q_ref, k_ref, v_ref, a_ref, wup_ref, ab_ref = (next(it) for _ in range(6))
    else:
        q_ref, v_ref, f_ref, lb_ref = (next(it) for _ in range(4))
    yf_ref = gate_ref = ng_ref = None
    if final:
        yf_ref, gate_ref, ng_ref = (next(it) for _ in range(3))
    o_ref = next(it)
    st_ref = next(it)

    @pl.when(pl.program_id(1) == 0)
    def _():
        st_ref[...] = jnp.zeros_like(st_ref)

    mask = _tri_mask(q, reverse)
    n_sub = o_ref.shape[0] // q
    for sub in (reversed(range(n_sub)) if reverse else range(n_sub)):
        rows = slice(sub * q, (sub + 1) * q)
        if kind == "gla":
            logit = _dot(a_ref[rows, :], wup_ref[...]) + ab_ref[...]
            la = _log_sigmoid(logit) * (1.0 / GLA_TAU)
            qf = q_ref[rows, :].astype(F32) * (hk ** -0.5)
            kf = k_ref[rows, :].astype(F32)
        else:
            lb = lb_ref[...]
            f = lb + (1.0 - lb) * _sigmoid(f_ref[rows, :].astype(F32))
            la = jnp.log(f)
            kf = 1.0 - f
            qf = _silu(q_ref[rows, :].astype(F32))

        cum = _cumsum_rows(la)
        total = cum[q - 1:q, :]
        if reverse:
            cum = total - cum + la
            ref = cum[q // 2 - 1:q // 2, :]
        else:
            ref = cum[q // 2:q // 2 + 1, :]
        q_rel = qf * jnp.exp(cum - ref)
        k_rel = kf * jnp.exp(ref - cum)
        q_abs = (q_rel * jnp.exp(ref)).astype(BF16)
        k_out = (k_rel * jnp.exp(total - ref)).astype(BF16)
        q_rel = q_rel.astype(BF16)
        k_rel = k_rel.astype(BF16)
        e_tot = jnp.exp(total)

        for h in range(heads):
            ks = slice(h * hk, (h + 1) * hk)
            vs = slice(h * hv, (h + 1) * hv)
            v = v_ref[rows, vs]
            scores = jnp.where(mask, _dot_nt(q_rel[:, ks], k_rel[:, ks]), 0.0)
            st = st_ref[h]
            y = _dot(scores.astype(BF16), v) + _dot_nt(q_abs[:, ks], st.astype(BF16))
            st_ref[h] = st * e_tot[:, ks] + _dot_tn(v, k_out[:, ks])
            if final:
                y = y + yf_ref[rows, vs].astype(F32)
                ms = jnp.mean(y * y, axis=1, keepdims=True)
                y = y * lax.rsqrt(ms + EPS) * ng_ref[...] * _silu(gate_ref[rows, vs].astype(F32))
            o_ref[rows, vs] = y.astype(o_ref.dtype)


def _vscan(lay, kind, proj, extra, norm_g, heads, hk, hv, reverse, yf):
    q = 64
    rows = 2 * q
    m = proj.shape[0]
    nsteps, row_block, _ = _scan_blocks(lay, rows, reverse)
    final = yf is not None
    kd, vd = heads * hk, heads * hv
    tok = lambda w, cb: pl.BlockSpec((rows, w), lambda b, s: (row_block(b, s), cb))
    const = lambda shape, cb: pl.BlockSpec(shape, lambda b, s: (0, cb))
    d = 1 if reverse else 0
    if kind == "gla":
        a_low, wup, ab = extra
        in_specs = [tok(kd, 0), tok(kd, 1), tok(vd, 2 * kd // vd), tok(LANES, 0),
                    const((LANES, kd), d), const((1, kd), d)]
        args = [proj, proj, proj, a_low, wup, ab]
        gate_cb = 2 * kd // vd + 1
    else:
        (lb,) = extra
        in_specs = [tok(kd, 0), tok(vd, kd // vd), tok(kd, (kd + 2 * vd) // kd + d), const((1, kd), 0)]
        args = [proj, proj, proj, lb]
        gate_cb = kd // vd + 1
    if final:
        in_specs += [tok(vd, 0), tok(vd, gate_cb), const((1, hv), 0)]
        args += [yf, proj, norm_g.reshape(1, hv)]
    kern = functools.partial(_vscan_kernel, kind=kind, q=q, reverse=reverse, heads=heads, hk=hk, hv=hv,
                             final=final)
    return pl.pallas_call(
        kern,
        grid=(lay.batch, nsteps),
        in_specs=in_specs,
        out_specs=tok(vd, 0),
        out_shape=jax.ShapeDtypeStruct((m, vd), BF16),
        scratch_shapes=[pltpu.VMEM((heads, hv, hk), F32)],
        compiler_params=_cparams("parallel", "arbitrary"),
        name=f"{kind}_scan_{'bwd' if reverse else 'fwd'}",
    )(*args)


def _lower_bound_kernel(x_ref, o_ref):
    x = x_ref[...]
    mx = jnp.max(x, axis=0, keepdims=True)
    e = jnp.exp(x - mx)
    p = e / jnp.sum(e, axis=0, keepdims=True)
    acc = jnp.zeros_like(p[0:1])
    for i in range(x.shape[0]):
        acc = acc + p[i:i + 1]
        o_ref[i:i + 1, :] = acc - p[0:1]


def _lower_bounds(logits):
    return pl.pallas_call(_lower_bound_kernel, out_shape=jax.ShapeDtypeStruct(logits.shape, F32),
                          name="hgrn_lower_bounds")(logits)


def _conv_table(taps, bias, rows):
    c = bias.shape[0]
    tab = jnp.zeros((16, c), F32)
    tab = tab.at[jnp.asarray(rows)].set(taps).at[9].set(bias)
    return tab


def _ssd_mixer(lay, u, w_in, conv_w, conv_b, dt_bias, a_log, d_skip, d_model):
    d_inner = 2 * d_model
    heads = d_inner // SSD_HEAD_DIM
    conv_ch = d_inner + 2 * SSD_GROUPS * SSD_STATE
    cw = _conv_table(conv_w, conv_b, (3, 4, 5))
    steps = 8
    xbc, zproj = _proj_conv(lay, u, w_in[:, d_inner:d_inner + conv_ch].astype(BF16), cw,
                            w_side=w_in[:, :d_inner].astype(BF16), tn=conv_ch // steps, hr=2 * SUBLANES,
                            vertical=False, lat_period=lay.seq, name="ssd_in_proj_conv")
    dtproj = _matmul(lay, u, w_in, d_inner + conv_ch, 2 * heads, out_dtype=F32)
    par = jnp.concatenate([dt_bias.reshape(1, 2 * heads), -jnp.exp(a_log.astype(F32)).reshape(1, 2 * heads),
                           jnp.zeros((6, 2 * heads), F32)], axis=0)
    dskip = jnp.repeat(d_skip, SSD_HEAD_DIM).reshape(1, d_inner)
    yf = _ssd_scan(lay, xbc, zproj, dtproj, par, dskip, d_inner, False, None)
    return _ssd_scan(lay, xbc, zproj, dtproj, par, dskip, d_inner, True, yf)


def _deinterleave_heads(w, heads):
    d, n = w.shape
    hk = n // heads
    return w.reshape(d, heads, hk // 2, 2).transpose(0, 1, 3, 2).reshape(d, n)


def _ret_mixer(lay, u, w_in, decay_logit, gn_g, gn_b, d_model):
    qk, vd = d_model, 2 * d_model
    hk = qk // RET_HEADS
    w_qk = _deinterleave_heads(w_in[:, :2 * qk], 2 * RET_HEADS).astype(BF16)
    qkp = _matmul(lay, u, w_qk, 0, 2 * qk)
    vgp = _matmul(lay, u, w_in, 2 * qk, 2 * vd)
    half = hk // 2
    inv_freq = 1.0 / (10000.0 ** jnp.linspace(0.0, 1.0, half, dtype=F32))
    ang = jnp.arange(lay.ctx + lay.seq, dtype=F32)[:, None] * inv_freq[None, :]
    cos, sin = jnp.cos(ang), jnp.sin(ang)
    dl = jnp.repeat(decay_logit.astype(F32), LANES, axis=1)
    dl = jnp.concatenate([dl, jnp.zeros((6, dl.shape[1]), F32)], axis=0)
    yf = _ret_scan(lay, qkp, vgp, cos, sin, dl, gn_g, gn_b, qk, vd, False, None)
    return _ret_scan(lay, qkp, vgp, cos, sin, dl, gn_g, gn_b, qk, vd, True, yf)


def _gla_mixer(lay, u, w_in, w_alpha_up, alpha_b, norm_g, d_model):
    kd, vd = d_model // 2, d_model
    n_main = 2 * kd + 2 * vd
    proj = _matmul(lay, u, w_in, 0, n_main)
    w_low = jnp.pad(w_in[:, n_main:], ((0, 0), (0, LANES - 2 * GLA_RANK))).astype(BF16)
    a_low = _matmul(lay, u, w_low, 0, LANES)
    wup = jnp.zeros((LANES, 2 * kd), F32)
    wup = wup.at[0:GLA_RANK, 0:kd].set(w_alpha_up[0]).at[GLA_RANK:2 * GLA_RANK, kd:].set(w_alpha_up[1])
    ab = alpha_b.reshape(1, 2 * kd)
    args = (lay, "gla", proj, (a_low, wup.astype(BF16), ab), norm_g, GLA_HEADS, kd // GLA_HEADS,
            vd // GLA_HEADS)
    yf = _vscan(*args, False, None)
    return _vscan(*args, True, yf)


def _hgrn_mixer(lay, u, w_in, lower_bound, norm_g, d_model):
    heads = d_model // HGRN_HEAD_K
    proj = _matmul(lay, u, w_in, 0, w_in.shape[1])
    args = (lay, "hgrn", proj, (lower_bound.reshape(1, d_model),), norm_g, heads, HGRN_HEAD_K,
            d_model // heads)
    yf = _vscan(*args, False, None)
    return _vscan(*args, True, yf)


def kernel(x, c, ctx, c_ctx, mod_w, mod_b, ln_mix_g, ln_mix_b, ln_ffn_g, ln_ffn_b, ffn_w_up, ffn_conv_w, ffn_conv_b, ffn_w_down, hgrn_lb_logits, ssd_w_in, ssd_conv_w, ssd_conv_b, ssd_dt_bias, ssd_a_log, ssd_d, ssd_norm_g, ssd_w_out, ret_w_in, ret_decay_logit, ret_gn_g, ret_gn_b, ret_w_out, gla_w_in, gla_w_alpha_up, gla_alpha_b, gla_norm_g, gla_w_out, hgrn_w_in, hgrn_norm_g, hgrn_w_out):
    batch, seq, d = x.shape
    ctx_len = ctx.shape[1]
    depth = mod_w.shape[0]
    hidden = ffn_w_down.shape[1]
    lay = _Layout(batch, seq, ctx_len)
    alpha = (2.0 * depth) ** 0.25
    n_mixers = 4
    assert GRID_W & (GRID_W - 1) == 0 and ctx_len & (ctx_len - 1) == 0 and seq & (seq - 1) == 0

    c_rows = jnp.concatenate([c, c_ctx[None], jnp.zeros((SUBLANES - batch - 1, d), F32)], axis=0)
    mods = _modulation(c_rows, mod_w, mod_b).reshape(depth, SUBLANES, 1, 6 * d)
    SH_M, SC_M, G_M, SH_F, SC_F, G_F = range(6)

    lower_bounds = _lower_bounds(hgrn_lb_logits.astype(F32))

    h, u = _modulate(lay, x.reshape(lay.m_lat, d), ctx.reshape(lay.m_ctx, d), mods, 0, SC_M, SH_M)

    ffn_tn = 512
    hp = -(-hidden // ffn_tn) * ffn_tn
    tk_ffn = hp // 4 if (hp // 4) % LANES == 0 else ffn_tn
    pad = hp - hidden
    w_up_all = jnp.pad(ffn_w_up.reshape(depth, d, 2, hidden).transpose(0, 2, 1, 3),
                       ((0, 0), (0, 0), (0, 0), (0, pad))).astype(BF16)
    w_down_all = jnp.pad(ffn_w_down, ((0, 0), (0, pad), (0, 0))).astype(BF16)
    cw_all = jnp.zeros((depth, 16, hp), F32)
    cw_all = cw_all.at[:, 0:9, :hidden].set(ffn_conv_w.reshape(depth, 9, hidden))
    cw_all = cw_all.at[:, 9, :hidden].set(ffn_conv_b)
    for i in range(depth):
        kind, j = i % n_mixers, i // n_mixers
        last = i == depth - 1
        m_rows = lay.m_lat if last else None
        rms_gain = None
        if kind == 0:
            y = _ssd_mixer(lay, u, ssd_w_in[j], ssd_conv_w[j], ssd_conv_b[j], ssd_dt_bias[j], ssd_a_log[j],
                           ssd_d[j], d)
            w_out, rms_gain = ssd_w_out[j], ssd_norm_g[j]
        elif kind == 1:
            y = _ret_mixer(lay, u, ret_w_in[j], ret_decay_logit[j], ret_gn_g[j], ret_gn_b[j], d)
            w_out = ret_w_out[j]
        elif kind == 2:
            y = _gla_mixer(lay, u, gla_w_in[j], gla_w_alpha_up[j], gla_alpha_b[j], gla_norm_g[j], d)
            w_out = gla_w_out[j]
        else:
            y = _hgrn_mixer(lay, u, hgrn_w_in[j], lower_bounds[i], hgrn_norm_g[j], d)
            w_out = hgrn_w_out[j]
        kd = w_out.shape[0]
        h, u = _matmul_ln(lay, y, w_out.astype(BF16), h, mods, i, G_M, ln_mix_g[i], ln_mix_b[i], alpha,
                          next_mod=(i, SC_F, SH_F), rms_gain=rms_gain, tk=min(kd, 2048), m_rows=m_rows)

        act = _proj_conv(lay, u, w_up_all, cw_all, layer=i, tn=ffn_tn, hr=GRID_W, vertical=True,
                         lat_period=GRID_W, m_rows=m_rows, name="ffn_up_conv")
        nxt = (i + 1, SC_M, SH_M) if not last else None
        h, u = _matmul_ln(lay, act, w_down_all, h, mods, i, G_F, ln_ffn_g[i], ln_ffn_b[i], alpha,
                          next_mod=nxt, tk=tk_ffn, m_rows=m_rows, w_layer=i)
    return h.reshape(batch, seq, d)
```

```python
import functools

import jax
import jax.numpy as jnp
from jax import lax
from jax.experimental import pallas as pl
from jax.experimental.pallas import tpu as pltpu

F32 = jnp.float32
BF16 = jnp.bfloat16

EPS = 1e-5
GRID_W = 64
SSD_HEAD_DIM = 64
SSD_GROUPS = 8
SSD_STATE = 128
RET_HEADS = 8
GLA_HEADS = 4
GLA_RANK = 16
GLA_TAU = 16.0
HGRN_HEAD_K = 128
LANES = 128
SUBLANES = 8
VMEM_LIMIT = 56 * 1024 * 1024
NEG_BIG = -1e30


def _cparams(*sem):
    return pltpu.CompilerParams(dimension_semantics=sem, vmem_limit_bytes=VMEM_LIMIT)


def _sigmoid(x):
    return 1.0 / (1.0 + jnp.exp(-x))


def _silu(x):
    return x * _sigmoid(x)


def _softplus(x):
    return jnp.maximum(x, 0.0) + jnp.log(1.0 + jnp.exp(-jnp.abs(x)))


def _log_sigmoid(x):
    return jnp.minimum(x, 0.0) - jnp.log(1.0 + jnp.exp(-jnp.abs(x)))


def _pow2_tile(limit, *sizes):
    t = 1
    while t * 2 <= limit and all(s % (t * 2) == 0 for s in sizes):
        t *= 2
    return t


def _dot(a, b):
    return jnp.dot(a, b, preferred_element_type=F32)


def _dot_nt(a, b):
    return lax.dot_general(a, b, (((1,), (1,)), ((), ())), preferred_element_type=F32)


def _dot_tn(a, b):
    return lax.dot_general(a, b, (((0,), (0,)), ((), ())), preferred_element_type=F32)


class _Layout:
    def __init__(self, batch, seq, ctx):
        self.batch, self.seq, self.ctx = batch, seq, ctx
        self.m_lat = batch * seq
        self.m_ctx = batch * ctx
        self.m = self.m_lat + self.m_ctx
        self.tm = _pow2_tile(1024, seq, self.m_ctx)
        self.tm_ln = _pow2_tile(512, seq, self.m_ctx)

    def mod_row(self, tile, tm):
        start = tile * tm
        return jnp.where(start < self.m_lat, start // self.seq, self.batch)


def _mod_kernel(c_ref, w_ref, b_ref, o_ref):
    c = _silu(c_ref[...]).astype(BF16)
    o_ref[...] = _dot(c, w_ref[...].astype(BF16)) + b_ref[...]


def _modulation(c_rows, mod_w, mod_b):
    depth, d, n = mod_w.shape
    rows = c_rows.shape[0]
    tn = _pow2_tile(1024, n)
    return pl.pallas_call(
        _mod_kernel,
        grid=(depth, n // tn),
        in_specs=[pl.BlockSpec((rows, d), lambda l, j: (0, 0)),
                  pl.BlockSpec((None, d, tn), lambda l, j: (l, 0, j)),
                  pl.BlockSpec((None, 1, tn), lambda l, j: (l, 0, j))],
        out_specs=pl.BlockSpec((None, rows, tn), lambda l, j: (l, 0, j)),
        out_shape=jax.ShapeDtypeStruct((depth, rows, n), F32),
        compiler_params=_cparams("parallel", "parallel"),
        name="modulation",
    )(c_rows, mod_w, mod_b.reshape(depth, 1, n))


def _cast_pad_kernel(x_ref, o_ref):
    r, c = x_ref.shape
    ro, co = o_ref.shape
    o_ref[0:r, 0:c] = x_ref[...].astype(o_ref.dtype)
    if co > c:
        o_ref[0:r, c:co] = jnp.zeros((r, co - c), o_ref.dtype)
    if ro > r:
        o_ref[r:ro, :] = jnp.zeros((ro - r, co), o_ref.dtype)


def _cast_pad(w, in_block, out_block):
    depth, rows, cols = w.shape
    gr, gc = rows // in_block[0], cols // in_block[1]
    return pl.pallas_call(
        _cast_pad_kernel,
        grid=(depth, gr, gc),
        in_specs=[pl.BlockSpec((None,) + in_block, lambda l, i, j: (l, i, j))],
        out_specs=pl.BlockSpec((None,) + out_block, lambda l, i, j: (l, i, j)),
        out_shape=jax.ShapeDtypeStruct((depth, gr * out_block[0], gc * out_block[1]), BF16),
        compiler_params=_cparams("parallel", "parallel", "parallel"),
        name="weight_cast_pad",
    )(w)


def _modulate_kernel(x_ref, c_ref, sc_ref, sh_ref, h_ref, u_ref, *, n_lat_tiles):
    def emit(src):
        h = src[...]
        h_ref[...] = h
        u_ref[...] = (h * (1.0 + sc_ref[...]) + sh_ref[...]).astype(BF16)

    pl.when(pl.program_id(0) < n_lat_tiles)(lambda: emit(x_ref))
    pl.when(pl.program_id(0) >= n_lat_tiles)(lambda: emit(c_ref))


def _modulate(lay, x2, c2, mods, layer, sc_blk, sh_blk):
    d = x2.shape[1]
    tm = lay.tm_ln
    nl = lay.m_lat // tm
    nc = lay.m_ctx // tm
    mod_spec = lambda blk: pl.BlockSpec((None, None, 1, d), lambda i: (layer, lay.mod_row(i, tm), 0, blk))
    tok = pl.BlockSpec((tm, d), lambda i: (i, 0))
    return pl.pallas_call(
        functools.partial(_modulate_kernel, n_lat_tiles=nl),
        grid=(nl + nc,),
        in_specs=[pl.BlockSpec((tm, d), lambda i: (jnp.minimum(i, nl - 1), 0)),
                  pl.BlockSpec((tm, d), lambda i: (jnp.maximum(i - nl, 0), 0)),
                  mod_spec(sc_blk), mod_spec(sh_blk)],
        out_specs=[tok, tok],
        out_shape=[jax.ShapeDtypeStruct((lay.m, d), F32), jax.ShapeDtypeStruct((lay.m, d), BF16)],
        compiler_params=_cparams("parallel"),
        name="modulate",
    )(x2, c2, mods, mods)


def _mm_kernel(x_ref, w_ref, o_ref, *scratch, cast_w):
    if cast_w:
        wb_ref, = scratch

        @pl.when(pl.program_id(1) == 0)
        def _():
            wb_ref[...] = w_ref[...].astype(BF16)

        w = wb_ref[...]
    else:
        w = w_ref[...]
    o_ref[...] = _dot(x_ref[...], w).astype(o_ref.dtype)


def _matmul(lay, x, w, col0, n, out_dtype=BF16):
    m, k = x.shape
    tm = lay.tm
    tn = _pow2_tile(1024, n, col0) if col0 else _pow2_tile(1024, n)
    cast_w = w.dtype != BF16
    cb0 = col0 // tn
    return pl.pallas_call(
        functools.partial(_mm_kernel, cast_w=cast_w),
        grid=(n // tn, m // tm),
        in_specs=[pl.BlockSpec((tm, k), lambda j, i: (i, 0)),
                  pl.BlockSpec((k, tn), lambda j, i: (0, cb0 + j))],
        out_specs=pl.BlockSpec((tm, tn), lambda j, i: (i, j)),
        out_shape=jax.ShapeDtypeStruct((m, n), out_dtype),
        scratch_shapes=[pltpu.VMEM((k, tn), BF16)] if cast_w else [],
        compiler_params=_cparams("parallel", "arbitrary"),
        name="in_proj",
    )(x, w)


def _mm_ln_kernel(*refs, nk, n_tiles, kdim, alpha, rms, emit_u):
    it = iter(refs)
    x_ref, w_ref, h_ref, gate_ref, lng_ref, lnb_ref = (next(it) for _ in range(6))
    sc_ref = sh_ref = rg_ref = u_ref = ssq_ref = None
    if emit_u:
        sc_ref, sh_ref = next(it), next(it)
    if rms:
        rg_ref = next(it)
    hout_ref = next(it)
    if emit_u:
        u_ref = next(it)
    acc_refs = (next(it), next(it))
    ssq_refs = (next(it), next(it)) if rms else (None, None)
    i = pl.program_id(0)
    k = pl.program_id(1)

    def product():
        x = x_ref[...]
        ssq = None
        if rms:
            xf = x.astype(F32)
            ssq = jnp.sum(xf * xf, axis=1, keepdims=True)
            x = (xf * rg_ref[...]).astype(BF16)
        return _dot(x, w_ref[...]), ssq

    @pl.when(jnp.logical_and(i == 0, k == 0))
    def _():
        acc_refs[1][...] = jnp.zeros_like(acc_refs[1])
        if rms:
            ssq_refs[1][...] = jnp.zeros_like(ssq_refs[1])

    for slot in (0, 1):
        acc_ref, ssq_ref = acc_refs[slot], ssq_refs[slot]
        prev_acc, prev_ssq = acc_refs[1 - slot], ssq_refs[1 - slot]

        @pl.when(jnp.logical_and(k == 0, i % 2 == slot))
        def _(acc_ref=acc_ref, ssq_ref=ssq_ref, prev_acc=prev_acc, prev_ssq=prev_ssq):
            o_new, ssq_new = product()
            acc_ref[...] = o_new
            if rms:
                ssq_ref[...] = ssq_new
            o = prev_acc[...]
            if rms:
                o = o * lax.rsqrt(prev_ssq[...] * (1.0 / kdim) + EPS)
            y = alpha * h_ref[...] + gate_ref[...] * o
            mu = jnp.mean(y, axis=1, keepdims=True)
            yc = y - mu
            var = jnp.mean(yc * yc, axis=1, keepdims=True)
            hn = yc * lax.rsqrt(var + EPS) * lng_ref[...] + lnb_ref[...]
            hout_ref[...] = hn
            if emit_u:
                u_ref[...] = (hn * (1.0 + sc_ref[...]) + sh_ref[...]).astype(BF16)

        if nk > 1:
            @pl.when(jnp.logical_and(jnp.logical_and(k > 0, i < n_tiles), i % 2 == slot))
            def _(acc_ref=acc_ref, ssq_ref=ssq_ref):
                o_new, ssq_new = product()
                acc_ref[...] += o_new
                if rms:
                    ssq_ref[...] += ssq_new


def _matmul_ln(lay, x, w, h, mods, gate_layer, gate_blk, ln_g, ln_b, alpha, *,
               next_mod=None, rms_gain=None, tk, m_rows=None, w_layer=None):
    kdim = x.shape[1]
    m = x.shape[0] if m_rows is None else m_rows
    d = w.shape[-1]
    tm = lay.tm_ln
    nk = kdim // tk
    n_tiles = m // tm
    emit_u = next_mod is not None
    rms = rms_gain is not None
    cur = lambda i: jnp.minimum(i, n_tiles - 1)
    prev = lambda i: jnp.maximum(i - 1, 0)

    def mod_spec(layer, blk):
        return pl.BlockSpec((None, None, 1, d), lambda i, k: (layer, lay.mod_row(prev(i), tm), 0, blk))

    row_spec = pl.BlockSpec((1, d), lambda i, k: (0, 0))
    w_spec = (pl.BlockSpec((tk, d), lambda i, k: (k, 0)) if w_layer is None else
              pl.BlockSpec((None, tk, d), lambda i, k: (w_layer, k, 0)))
    in_specs = [pl.BlockSpec((tm, tk), lambda i, k: (cur(i), k)),
                w_spec,
                pl.BlockSpec((tm, d), lambda i, k: (prev(i), 0)),
                mod_spec(gate_layer, gate_blk), row_spec, row_spec]
    args = [x, w, h, mods, ln_g.reshape(1, d), ln_b.reshape(1, d)]
    if emit_u:
        nl, sc_blk, sh_blk = next_mod
        in_specs += [mod_spec(nl, sc_blk), mod_spec(nl, sh_blk)]
        args += [mods, mods]
    if rms:
        in_specs.append(pl.BlockSpec((1, tk), lambda i, k: (0, k)))
        args.append(rms_gain.reshape(1, kdim))
    out_specs = [pl.BlockSpec((tm, d), lambda i, k: (prev(i), 0))]
    out_shape = [jax.ShapeDtypeStruct((m, d), F32)]
    if emit_u:
        out_specs.append(pl.BlockSpec((tm, d), lambda i, k: (prev(i), 0)))
        out_shape.append(jax.ShapeDtypeStruct((m, d), BF16))
    scratch = [pltpu.VMEM((tm, d), F32), pltpu.VMEM((tm, d), F32)]
    if rms:
        scratch += [pltpu.VMEM((tm, 1), F32), pltpu.VMEM((tm, 1), F32)]
    outs = pl.pallas_call(
        functools.partial(_mm_ln_kernel, nk=nk, n_tiles=n_tiles, kdim=kdim, alpha=alpha, rms=rms,
                          emit_u=emit_u),
        grid=(n_tiles + 1, nk),
        in_specs=in_specs, out_specs=out_specs, out_shape=out_shape, scratch_shapes=scratch,
        compiler_params=_cparams("arbitrary", "arbitrary"),
        name="out_proj_ln",
    )(*args)
    return (outs[0], outs[1]) if emit_u else (outs[0], None)


_EXT_PAD = SUBLANES


def _proj_conv_kernel(*refs, tm, hr, vertical, second, rb, tiles_per_img, n_lat_tiles,
                      lat_period, ctx_len):
    it = iter(refs)
    xp_ref, x_ref, xn_ref, wg_ref = (next(it) for _ in range(4))
    wu_ref = next(it) if second else None
    cw_ref, o_ref = next(it), next(it)
    o2_ref = next(it) if second == "out" else None
    ext_ref = next(it)
    up_ref = next(it) if second == "mul" else None
    has_up = second == "mul"
    i = pl.program_id(1)
    wg = wg_ref[...]
    is_lat = i < n_lat_tiles
    period = jnp.where(is_lat, lat_period, ctx_len)
    tn = o_ref.shape[1]
    base = _EXT_PAD + hr
    vert = is_lat.astype(F32)
    tap_rows = (0, 1, 2) if vertical else (1,)

    prev = _dot(xp_ref[...], wg)
    nxt = _dot(xn_ref[...], wg)
    if vertical:
        t_img = i % tiles_per_img
        prev = prev * jnp.logical_and(is_lat, t_img > 0).astype(F32)
        nxt = nxt * jnp.logical_and(is_lat, t_img < tiles_per_img - 1).astype(F32)
    zpad = jnp.zeros((_EXT_PAD, tn), F32)
    ext_ref[0:_EXT_PAD, :] = zpad
    ext_ref[_EXT_PAD:base, :] = prev
    ext_ref[base:base + tm, :] = _dot(x_ref[...], wg)
    ext_ref[base + tm:base + tm + hr, :] = nxt
    ext_ref[base + tm + hr:base + tm + hr + _EXT_PAD, :] = zpad
    if has_up:
        up_ref[...] = _dot(x_ref[...], wu_ref[...])
    elif second == "out":
        o2_ref[...] = _dot(x_ref[...], wu_ref[...]).astype(o2_ref.dtype)

    cw = cw_ref[...]
    taps = {a: [cw[3 * a + b:3 * a + b + 1, :] * (1.0 if a == 1 else vert) for b in range(3)]
            for a in tap_rows}
    bias = cw[9:10, :]
    hb = _EXT_PAD

    for blk in range(tm // rb):
        r0 = blk * rb
        col = (lax.broadcasted_iota(jnp.int32, (rb, tn), 0) + (i * tm + r0)) & (period - 1)
        s_left = s_mid = s_right = None
        for a in tap_rows:
            lo = base + r0 + (a - 1) * hr - hb
            e = ext_ref[lo:lo + rb + 2 * hb, :]
            l, c, r = taps[a][0] * e, taps[a][1] * e[hb:hb + rb], taps[a][2] * e
            s_left = l if s_left is None else s_left + l
            s_mid = c if s_mid is None else s_mid + c
            s_right = r if s_right is None else s_right + r
        left = pltpu.roll(s_left, 1, 0)[hb:hb + rb]
        right = pltpu.roll(s_right, rb + 2 * hb - 1, 0)[hb:hb + rb]
        acc = (s_mid + bias) + jnp.where(col == 0, 0.0, left) + jnp.where(col == period - 1, 0.0, right)
        act = _silu(acc)
        if has_up:
            act = act * up_ref[r0:r0 + rb, :]
        o_ref[r0:r0 + rb, :] = act.astype(BF16)


def _proj_conv(lay, u, w, cw, *, layer=None, w_side=None, tn, hr, vertical, lat_period, m_rows=None,
               name):
    d = u.shape[1]
    m = u.shape[0] if m_rows is None else m_rows
    n = w.shape[-1]
    tm = lay.tm
    hpt = tm // hr
    last_halo = u.shape[0] // hr - 1
    second = "mul" if layer is not None else ("out" if w_side is not None else None)
    if second == "mul":
        n //= 2
    steps = n // tn
    tn2 = tn if second == "mul" else (w_side.shape[1] // steps if second == "out" else 0)
    kern = functools.partial(_proj_conv_kernel, tm=tm, hr=hr, vertical=vertical, second=second,
                             rb=min(tm, 128), tiles_per_img=lay.seq // tm,
                             n_lat_tiles=lay.m_lat // tm, lat_period=lat_period, ctx_len=lay.ctx)
    in_specs = [pl.BlockSpec((hr, d), lambda j, i: (jnp.maximum(i * hpt - 1, 0), 0)),
                pl.BlockSpec((tm, d), lambda j, i: (i, 0)),
                pl.BlockSpec((hr, d), lambda j, i: (jnp.minimum((i + 1) * hpt, last_halo), 0))]
    args = [u, u, u]
    tile = lambda width: pl.BlockSpec((tm, width), lambda j, i: (i, j))
    out_specs = [tile(tn)]
    out_shape = [jax.ShapeDtypeStruct((m, n), BF16)]
    scratch = [pltpu.VMEM((tm + 2 * hr + 2 * _EXT_PAD, tn), F32)]
    if second == "mul":
        in_specs += [pl.BlockSpec((None, d, tn), lambda j, i: (layer, 0, j)),
                     pl.BlockSpec((None, d, tn), lambda j, i: (layer, 0, steps + j)),
                     pl.BlockSpec((None, 16, tn), lambda j, i: (layer, 0, j))]
        args += [w, w, cw]
        scratch.append(pltpu.VMEM((tm, tn), F32))
    else:
        in_specs.append(pl.BlockSpec((d, tn), lambda j, i: (0, j)))
        args.append(w)
        if second == "out":
            assert w_side.shape[1] == steps * tn2 and tn2 % LANES == 0
            in_specs.append(pl.BlockSpec((d, tn2), lambda j, i: (0, j)))
            args.append(w_side)
            out_specs.append(tile(tn2))
            out_shape.append(jax.ShapeDtypeStruct((m, w_side.shape[1]), BF16))
        in_specs.append(pl.BlockSpec((16, tn), lambda j, i: (0, j)))
        args.append(cw)
    outs = pl.pallas_call(
        kern,
        grid=(steps, m // tm),
        in_specs=in_specs, out_specs=out_specs, out_shape=out_shape, scratch_shapes=scratch,
        compiler_params=_cparams("parallel", "parallel"),
        name=name,
    )(*args)
    return outs if second == "out" else outs[0]


def _scan_blocks(lay, q, reverse):
    nctx, nlat = lay.ctx // q, lay.seq // q
    ctx0 = lay.m_lat // q

    def row_block(b, s):
        cj = (nctx - 1 - s) if reverse else s
        lj = (nlat - 1 - (s - nctx)) if reverse else (s - nctx)
        return jnp.where(s < nctx, ctx0 + b * nctx + cj, b * nlat + lj)

    def pos_block(b, s):
        if not reverse:
            return s
        return jnp.where(s < nctx, nctx - 1 - s, nctx + nlat - 1 - (s - nctx))

    return nctx + nlat, row_block, pos_block


def _cumsum_rows(x):
    n = x.shape[0]
    rows = lax.broadcasted_iota(jnp.int32, x.shape, 0)
    shift = 1
    while shift < n:
        x = x + jnp.where(rows >= shift, pltpu.roll(x, shift, 0), 0.0)
        shift *= 2
    return x


def _tri_mask(q, reverse):
    ii = lax.broadcasted_iota(jnp.int32, (q, q), 0)
    jj = lax.broadcasted_iota(jnp.int32, (q, q), 1)
    return (ii <= jj) if reverse else (ii >= jj)


def _ssd_scan_kernel(*refs, q, reverse, n_pairs, final):
    it = iter(refs)
    xs_refs = (next(it), next(it))
    bc_ref, dt_ref, par_ref = next(it), next(it), next(it)
    yf_ref = z_refs = dsk_ref = None
    if final:
        yf_ref = next(it)
        z_refs = (next(it), next(it))
        dsk_ref = next(it)
    o_ref = next(it)
    s_ref = next(it)

    @pl.when(pl.program_id(1) == 0)
    def _():
        s_ref[...] = jnp.zeros_like(s_ref)

    lane0 = 64 if reverse else 0
    dt = _softplus(dt_ref[...] + par_ref[0:1, :])
    la = dt * par_ref[1:2, :]
    cum = _cumsum_rows(la)
    total = cum[q - 1:q, :]
    if reverse:
        cum = total - cum + la
    cum_t = cum.T
    dt_t = dt.T
    wout_t = (jnp.exp(total - cum) * dt).T
    e_tot = jnp.exp(total)
    mask = _tri_mask(q, reverse)
    lane = lax.broadcasted_iota(jnp.int32, (1, LANES), 1)
    lo_half = lane < SSD_HEAD_DIM
    lo_sel = jnp.where(lo_half, 1.0, 0.0).astype(BF16)
    hi_sel = jnp.where(lo_half, 0.0, 1.0).astype(BF16)
    half = n_pairs // 2
    pairs_per_group = n_pairs // SSD_GROUPS

    for g in range(SSD_GROUPS):
        k = bc_ref[:, g * SSD_STATE:(g + 1) * SSD_STATE]
        qm = bc_ref[:, (SSD_GROUPS + g) * SSD_STATE:(SSD_GROUPS + g + 1) * SSD_STATE]
        scores = _dot_nt(qm, k)
        k_t = k.astype(F32).T
        for p in range(pairs_per_group):
            pair = g * pairs_per_group + p
            xr = xs_refs[pair // half]
            c0 = (pair % half) * LANES
            xs = xr[:, c0:c0 + LANES]
            xs_half = (xs * lo_sel, xs * hi_sel)
            s_pair = s_ref[pair]
            y = jnp.zeros((q, LANES), F32)
            upd = jnp.zeros((SSD_STATE, LANES), F32)
            hs = (lane0 + 2 * pair, lane0 + 2 * pair + 1)
            e_in = []
            for hh, xh in zip(hs, xs_half):
                c_col = jnp.broadcast_to(cum[:, hh:hh + 1], (q, LANES))
                e_in.append(jnp.exp(c_col))
                decay = jnp.exp(jnp.where(mask, c_col - cum_t[hh:hh + 1, :], NEG_BIG))
                wts = (scores * decay * dt_t[hh:hh + 1, :]).astype(BF16)
                y = y + _dot(wts, xh)
                upd = upd + _dot((k_t * wout_t[hh:hh + 1, :]).astype(BF16), xh)
            y = y + jnp.where(lo_half, e_in[0], e_in[1]) * _dot(qm, s_pair.astype(BF16))
            e_tot_pair = jnp.where(lo_half, e_tot[:, hs[0]:hs[0] + 1], e_tot[:, hs[1]:hs[1] + 1])
            s_ref[pair] = s_pair * e_tot_pair + upd
            oc = slice(pair * LANES, (pair + 1) * LANES)
            if final:
                y = y + yf_ref[:, oc].astype(F32)
                y = y + dsk_ref[:, oc] * xs.astype(F32)
                y = y * _silu(z_refs[pair // half][:, c0:c0 + LANES].astype(F32))
            o_ref[:, oc] = y.astype(o_ref.dtype)


def _ssd_scan(lay, xbc, zproj, dtproj, par, dskip, d_inner, reverse, yf):
    q = LANES
    m = xbc.shape[0]
    nsteps, row_block, _ = _scan_blocks(lay, q, reverse)
    hw = d_inner // 2
    assert hw == 2 * SSD_GROUPS * SSD_STATE
    n_pairs = d_inner // LANES
    final = yf is not None
    tok = lambda cb: (lambda b, s: (row_block(b, s), cb))
    in_specs = [pl.BlockSpec((q, hw), tok(0)), pl.BlockSpec((q, hw), tok(1)),
                pl.BlockSpec((q, hw), tok(2)),
                pl.BlockSpec((q, LANES), tok(0)),
                pl.BlockSpec((8, LANES), lambda b, s: (0, 0))]
    args = [xbc, xbc, xbc, dtproj, par]
    if final:
        in_specs += [pl.BlockSpec((q, d_inner), tok(0)),
                     pl.BlockSpec((q, hw), tok(0)), pl.BlockSpec((q, hw), tok(1)),
                     pl.BlockSpec((1, d_inner), lambda b, s: (0, 0))]
        args += [yf, zproj, zproj, dskip]
    kern = functools.partial(_ssd_scan_kernel, q=q, reverse=reverse, n_pairs=n_pairs, final=final)
    return pl.pallas_call(
        kern,
        grid=(lay.batch, nsteps),
        in_specs=in_specs,
        out_specs=pl.BlockSpec((q, d_inner), tok(0)),
        out_shape=jax.ShapeDtypeStruct((m, d_inner), BF16),
        scratch_shapes=[pltpu.VMEM((n_pairs, SSD_STATE, LANES), F32)],
        compiler_params=_cparams("parallel", "arbitrary"),
        name="ssd_scan_bwd" if reverse else "ssd_scan_fwd",
    )(*args)


def _ret_scan_kernel(*refs, q, reverse, hk, hv, final):
    it = iter(refs)
    q_ref, k_ref, v_ref, cos_ref, sin_ref, dl_ref = (next(it) for _ in range(6))
    yf_ref = g_ref = gng_ref = gnb_ref = None
    if final:
        yf_ref, g_ref, gng_ref, gnb_ref = (next(it) for _ in range(4))
    o_ref = next(it)
    s_ref, dec_ref, ein_ref, eout_ref = (next(it) for _ in range(4))
    half = hk // 2
    row = 1 if reverse else 0

    @pl.when(pl.program_id(1) == 0)
    def _():
        s_ref[...] = jnp.zeros_like(s_ref)

    @pl.when(pl.program_id(1) == 0)
    def _():
        ii = lax.broadcasted_iota(jnp.int32, (q, q), 0)
        jj = lax.broadcasted_iota(jnp.int32, (q, q), 1)
        dist = ((jj - ii) if reverse else (ii - jj)).astype(F32)
        mask = _tri_mask(q, reverse)
        pos = lax.broadcasted_iota(jnp.int32, (q, LANES), 0).astype(F32)
        for h in range(RET_HEADS):
            lg = _log_sigmoid(dl_ref[row:row + 1, h * LANES:(h + 1) * LANES])
            dec_ref[h] = jnp.exp(jnp.where(mask, dist * lg[:, :q], NEG_BIG))
            if reverse:
                ein_ref[h] = jnp.exp((q - pos) * lg)
                eout_ref[h] = jnp.exp(pos * lg)
            else:
                ein_ref[h] = jnp.exp((pos + 1.0) * lg)
                eout_ref[h] = jnp.exp((q - 1.0 - pos) * lg)

    cos = cos_ref[...]
    sin = sin_ref[...]

    def rot(ref, h):
        x0 = ref[:, h * hk:h * hk + half].astype(F32)
        x1 = ref[:, h * hk + half:(h + 1) * hk].astype(F32)
        return x0 * cos - x1 * sin, x0 * sin + x1 * cos

    scale = hk ** -0.5
    for h in range(RET_HEADS):
        q0, q1 = rot(q_ref, h)
        k0, k1 = rot(k_ref, h)
        k0, k1 = k0 * scale, k1 * scale
        qr = jnp.concatenate([q0, q1], axis=1).astype(BF16)
        kr = jnp.concatenate([k0, k1], axis=1).astype(BF16)
        v = v_ref[:, h * hv:(h + 1) * hv]
        scores = _dot_nt(qr, kr)
        y = _dot((scores * dec_ref[h]).astype(BF16), v)
        st = s_ref[h]
        ein = ein_ref[h]
        inter = _dot(qr, st.astype(BF16))
        y = y + jnp.concatenate([ein] * (hv // LANES), axis=1) * inter
        eout = eout_ref[h]
        ks = jnp.concatenate([k0 * eout, k1 * eout], axis=1).astype(BF16)
        e_tot = ein[0:1, :] * eout[0:1, :]
        s_ref[h] = st * jnp.concatenate([e_tot] * (hv // LANES), axis=1) + _dot_tn(ks, v)
        if final:
            y = y + yf_ref[:, h * hv:(h + 1) * hv].astype(F32)
            mu = jnp.mean(y, axis=1, keepdims=True)
            yc = y - mu
            var = jnp.mean(yc * yc, axis=1, keepdims=True)
            yn = yc * lax.rsqrt(var + EPS) * gng_ref[:, h * hv:(h + 1) * hv] + gnb_ref[:, h * hv:(h + 1) * hv]
            y = _silu(g_ref[:, h * hv:(h + 1) * hv].astype(F32)) * yn
        o_ref[:, h * hv:(h + 1) * hv] = y.astype(o_ref.dtype)


def _ret_scan(lay, qk, vg, cos, sin, dl, gn_g, gn_b, qk_dim, v_dim, reverse, yf):
    q = LANES
    m = qk.shape[0]
    nsteps, row_block, pos_block = _scan_blocks(lay, q, reverse)
    final = yf is not None
    hk, hv = qk_dim // RET_HEADS, v_dim // RET_HEADS
    tok = lambda w, cb: pl.BlockSpec((q, w), lambda b, s: (row_block(b, s), cb))
    pos = pl.BlockSpec((q, hk // 2), lambda b, s: (pos_block(b, s), 0))
    const = lambda shape: pl.BlockSpec(shape, lambda b, s: (0, 0))
    in_specs = [tok(qk_dim, 0), tok(qk_dim, 1), tok(v_dim, 0), pos, pos, const(dl.shape)]
    args = [qk, qk, vg, cos, sin, dl]
    if final:
        in_specs += [tok(v_dim, 0), tok(v_dim, 1), const((1, v_dim)), const((1, v_dim))]
        args += [yf, vg, gn_g.reshape(1, v_dim), gn_b.reshape(1, v_dim)]
    kern = functools.partial(_ret_scan_kernel, q=q, reverse=reverse, hk=hk, hv=hv, final=final)
    return pl.pallas_call(
        kern,
        grid=(lay.batch, nsteps),
        in_specs=in_specs,
        out_specs=tok(v_dim, 0),
        out_shape=jax.ShapeDtypeStruct((m, v_dim), BF16),
        scratch_shapes=[pltpu.VMEM((RET_HEADS, hk, hv), F32), pltpu.VMEM((RET_HEADS, q, q), F32),
                        pltpu.VMEM((RET_HEADS, q, LANES), F32), pltpu.VMEM((RET_HEADS, q, LANES), F32)],
        compiler_params=_cparams("parallel", "arbitrary"),
        name="ret_scan_bwd" if reverse else "ret_scan_fwd",
    )(*args)


def _vscan_kernel(*refs, kind, q, reverse, heads, hk, hv, final):
    it = iter(refs)
    if kind == "gla":
        q_ref, k_ref, v_ref, a_ref, wup_ref, ab_ref = (next(it) for _ in range(6))
    else:
        q_ref, v_ref, f_ref, lb_ref = (next(it) for _ in range(4))
    yf_ref = gate_ref = ng_ref = None
    if final:
        yf_ref, gate_ref, ng_ref = (next(it) for _ in range(3))
    o_ref = next(it)
    st_ref = next(it)

    @pl.when(pl.program_id(1) == 0)
    def _():
        st_ref[...] = jnp.zeros_like(st_ref)

    mask = _tri_mask(q, reverse)
    n_sub = o_ref.shape[0] // q
    for sub in (reversed(range(n_sub)) if reverse else range(n_sub)):
        rows = slice(sub * q, (sub + 1) * q)
        if kind == "gla":
            logit = _dot(a_ref[rows, :], wup_ref[...]) + ab_ref[...]
            la = _log_sigmoid(logit) * (1.0 / GLA_TAU)
            qf = q_ref[rows, :].astype(F32) * (hk ** -0.5)
            kf = k_ref[rows, :].astype(F32)
        else:
            lb = lb_ref[...]
            f = lb + (1.0 - lb) * _sigmoid(f_ref[rows, :].astype(F32))
            la = jnp.log(f)
            kf = 1.0 - f
            qf = _silu(q_ref[rows, :].astype(F32))

        cum = _cumsum_rows(la)
        total = cum[q - 1:q, :]
        if reverse:
            cum = total - cum + la
            ref = cum[q // 2 - 1:q // 2, :]
        else:
            ref = cum[q // 2:q // 2 + 1, :]
        e_rel = jnp.exp(cum - ref)
        q_rel = qf * e_rel
        k_rel = kf * (1.0 / e_rel)
        q_abs = (q_rel * jnp.exp(ref)).astype(BF16)
        k_out = (k_rel * jnp.exp(total - ref)).astype(BF16)
        q_rel = q_rel.astype(BF16)
        k_rel = k_rel.astype(BF16)
        e_tot = jnp.exp(total)

        for h in range(heads):
            ks = slice(h * hk, (h + 1) * hk)
            vs = slice(h * hv, (h + 1) * hv)
            v = v_ref[rows, vs]
            scores = jnp.where(mask, _dot_nt(q_rel[:, ks], k_rel[:, ks]), 0.0)
            st = st_ref[h]
            y = _dot(scores.astype(BF16), v) + _dot_nt(q_abs[:, ks], st.astype(BF16))
            st_ref[h] = st * e_tot[:, ks] + _dot_tn(v, k_out[:, ks])
            if final:
                y = y + yf_ref[rows, vs].astype(F32)
                ms = jnp.mean(y * y, axis=1, keepdims=True)
                y = y * lax.rsqrt(ms + EPS) * ng_ref[...] * _silu(gate_ref[rows, vs].astype(F32))
            o_ref[rows, vs] = y.astype(o_ref.dtype)


def _vscan(lay, kind, proj, extra, norm_g, heads, hk, hv, reverse, yf):
    q = 64
    rows = 2 * q
    m = proj.shape[0]
    nsteps, row_block, _ = _scan_blocks(lay, rows, reverse)
    final = yf is not None
    kd, vd = heads * hk, heads * hv
    tok = lambda w, cb: pl.BlockSpec((rows, w), lambda b, s: (row_block(b, s), cb))
    const = lambda shape, cb: pl.BlockSpec(shape, lambda b, s: (0, cb))
    d = 1 if reverse else 0
    if kind == "gla":
        a_low, wup, ab = extra
        in_specs = [tok(kd, 0), tok(kd, 1), tok(vd, 2 * kd // vd), tok(LANES, 0),
                    const((LANES, kd), d), const((1, kd), d)]
        args = [proj, proj, proj, a_low, wup, ab]
        gate_cb = 2 * kd // vd + 1
    else:
        (lb,) = extra
        in_specs = [tok(kd, 0), tok(vd, kd // vd), tok(kd, (kd + 2 * vd) // kd + d), const((1, kd), 0)]
        args = [proj, proj, proj, lb]
        gate_cb = kd // vd + 1
    if final:
        in_specs += [tok(vd, 0), tok(vd, gate_cb), const((1, hv), 0)]
        args += [yf, proj, norm_g.reshape(1, hv)]
    kern = functools.partial(_vscan_kernel, kind=kind, q=q, reverse=reverse, heads=heads, hk=hk, hv=hv,
                             final=final)
    return pl.pallas_call(
        kern,
        grid=(lay.batch, nsteps),
        in_specs=in_specs,
        out_specs=tok(vd, 0),
        out_shape=jax.ShapeDtypeStruct((m, vd), BF16),
        scratch_shapes=[pltpu.VMEM((heads, hv, hk), F32)],
        compiler_params=_cparams("parallel", "arbitrary"),
        name=f"{kind}_scan_{'bwd' if reverse else 'fwd'}",
    )(*args)


def _lower_bound_kernel(x_ref, o_ref):
    x = x_ref[...]
    mx = jnp.max(x, axis=0, keepdims=True)
    e = jnp.exp(x - mx)
    p = e / jnp.sum(e, axis=0, keepdims=True)
    acc = jnp.zeros_like(p[0:1])
    for i in range(x.shape[0]):
        acc = acc + p[i:i + 1]
        o_ref[i:i + 1, :] = acc - p[0:1]


def _lower_bounds(logits):
    return pl.pallas_call(_lower_bound_kernel, out_shape=jax.ShapeDtypeStruct(logits.shape, F32),
                          name="hgrn_lower_bounds")(logits)


def _conv_table(taps, bias, rows):
    c = bias.shape[0]
    tab = jnp.zeros((16, c), F32)
    tab = tab.at[jnp.asarray(rows)].set(taps).at[9].set(bias)
    return tab


def _ssd_mixer(lay, u, w_in, conv_w, conv_b, dt_bias, a_log, d_skip, d_model):
    d_inner = 2 * d_model
    heads = d_inner // SSD_HEAD_DIM
    conv_ch = d_inner + 2 * SSD_GROUPS * SSD_STATE
    cw = _conv_table(conv_w, conv_b, (3, 4, 5))
    xbc = _proj_conv(lay, u, w_in[:, d_inner:d_inner + conv_ch].astype(BF16), cw, tn=1024,
                     hr=2 * SUBLANES, vertical=False, lat_period=lay.seq, name="ssd_in_proj_conv")
    zproj = _matmul(lay, u, w_in, 0, d_inner)
    dtproj = _matmul(lay, u, w_in, d_inner + conv_ch, 2 * heads, out_dtype=F32)
    par = jnp.concatenate([dt_bias.reshape(1, 2 * heads), -jnp.exp(a_log.astype(F32)).reshape(1, 2 * heads),
                           jnp.zeros((6, 2 * heads), F32)], axis=0)
    dskip = jnp.repeat(d_skip, SSD_HEAD_DIM).reshape(1, d_inner)
    yf = _ssd_scan(lay, xbc, zproj, dtproj, par, dskip, d_inner, False, None)
    return _ssd_scan(lay, xbc, zproj, dtproj, par, dskip, d_inner, True, yf)


def _deinterleave_heads(w, heads):
    d, n = w.shape
    hk = n // heads
    return w.reshape(d, heads, hk // 2, 2).transpose(0, 1, 3, 2).reshape(d, n)


def _ret_mixer(lay, u, w_in, decay_logit, gn_g, gn_b, d_model):
    qk, vd = d_model, 2 * d_model
    hk = qk // RET_HEADS
    w_qk = _deinterleave_heads(w_in[:, :2 * qk], 2 * RET_HEADS).astype(BF16)
    qkp = _matmul(lay, u, w_qk, 0, 2 * qk)
    vgp = _matmul(lay, u, w_in, 2 * qk, 2 * vd)
    half = hk // 2
    inv_freq = 1.0 / (10000.0 ** jnp.linspace(0.0, 1.0, half, dtype=F32))
    ang = jnp.arange(lay.ctx + lay.seq, dtype=F32)[:, None] * inv_freq[None, :]
    cos, sin = jnp.cos(ang), jnp.sin(ang)
    dl = jnp.repeat(decay_logit.astype(F32), LANES, axis=1)
    dl = jnp.concatenate([dl, jnp.zeros((6, dl.shape[1]), F32)], axis=0)
    yf = _ret_scan(lay, qkp, vgp, cos, sin, dl, gn_g, gn_b, qk, vd, False, None)
    return _ret_scan(lay, qkp, vgp, cos, sin, dl, gn_g, gn_b, qk, vd, True, yf)


def _gla_mixer(lay, u, w_in, w_alpha_up, alpha_b, norm_g, d_model):
    kd, vd = d_model // 2, d_model
    n_main = 2 * kd + 2 * vd
    proj = _matmul(lay, u, w_in, 0, n_main)
    w_low = jnp.pad(w_in[:, n_main:], ((0, 0), (0, LANES - 2 * GLA_RANK))).astype(BF16)
    a_low = _matmul(lay, u, w_low, 0, LANES)
    wup = jnp.zeros((LANES, 2 * kd), F32)
    wup = wup.at[0:GLA_RANK, 0:kd].set(w_alpha_up[0]).at[GLA_RANK:2 * GLA_RANK, kd:].set(w_alpha_up[1])
    ab = alpha_b.reshape(1, 2 * kd)
    args = (lay, "gla", proj, (a_low, wup.astype(BF16), ab), norm_g, GLA_HEADS, kd // GLA_HEADS,
            vd // GLA_HEADS)
    yf = _vscan(*args, False, None)
    return _vscan(*args, True, yf)


def _hgrn_mixer(lay, u, w_in, lower_bound, norm_g, d_model):
    heads = d_model // HGRN_HEAD_K
    proj = _matmul(lay, u, w_in, 0, w_in.shape[1])
    args = (lay, "hgrn", proj, (lower_bound.reshape(1, d_model),), norm_g, heads, HGRN_HEAD_K,
            d_model // heads)
    yf = _vscan(*args, False, None)
    return _vscan(*args, True, yf)


def kernel(x, c, ctx, c_ctx, mod_w, mod_b, ln_mix_g, ln_mix_b, ln_ffn_g, ln_ffn_b, ffn_w_up, ffn_conv_w, ffn_conv_b, ffn_w_down, hgrn_lb_logits, ssd_w_in, ssd_conv_w, ssd_conv_b, ssd_dt_bias, ssd_a_log, ssd_d, ssd_norm_g, ssd_w_out, ret_w_in, ret_decay_logit, ret_gn_g, ret_gn_b, ret_w_out, gla_w_in, gla_w_alpha_up, gla_alpha_b, gla_norm_g, gla_w_out, hgrn_w_in, hgrn_norm_g, hgrn_w_out):
    batch, seq, d = x.shape
    ctx_len = ctx.shape[1]
    depth = mod_w.shape[0]
    hidden = ffn_w_down.shape[1]
    lay = _Layout(batch, seq, ctx_len)
    alpha = (2.0 * depth) ** 0.25
    n_mixers = 4
    assert GRID_W & (GRID_W - 1) == 0 and ctx_len & (ctx_len - 1) == 0 and seq & (seq - 1) == 0

    c_rows = jnp.concatenate([c, c_ctx[None], jnp.zeros((SUBLANES - batch - 1, d), F32)], axis=0)
    mods = _modulation(c_rows, mod_w, mod_b).reshape(depth, SUBLANES, 1, 6 * d)
    SH_M, SC_M, G_M, SH_F, SC_F, G_F = range(6)

    lower_bounds = _lower_bounds(hgrn_lb_logits.astype(F32))

    h, u = _modulate(lay, x.reshape(lay.m_lat, d), ctx.reshape(lay.m_ctx, d), mods, 0, SC_M, SH_M)

    ffn_tn = 512
    hp = -(-hidden // ffn_tn) * ffn_tn
    tk_ffn = hp // 4 if (hp // 4) % LANES == 0 else ffn_tn
    pad = hp - hidden
    del pad
    w_up_all = _cast_pad(ffn_w_up, (d // 8, hidden), (d // 8, hp))
    w_down_all = _cast_pad(ffn_w_down, (hidden, d // 8), (hp, d // 8))
    cw_all = jnp.zeros((depth, 16, hp), F32)
    cw_all = cw_all.at[:, 0:9, :hidden].set(ffn_conv_w.reshape(depth, 9, hidden))
    cw_all = cw_all.at[:, 9, :hidden].set(ffn_conv_b)
    for i in range(depth):
        kind, j = i % n_mixers, i // n_mixers
        last = i == depth - 1
        m_rows = lay.m_lat if last else None
        rms_gain = None
        if kind == 0:
            y = _ssd_mixer(lay, u, ssd_w_in[j], ssd_conv_w[j], ssd_conv_b[j], ssd_dt_bias[j], ssd_a_log[j],
                           ssd_d[j], d)
            w_out, rms_gain = ssd_w_out[j], ssd_norm_g[j]
        elif kind == 1:
            y = _ret_mixer(lay, u, ret_w_in[j], ret_decay_logit[j], ret_gn_g[j], ret_gn_b[j], d)
            w_out = ret_w_out[j]
        elif kind == 2:
            y = _gla_mixer(lay, u, gla_w_in[j], gla_w_alpha_up[j], gla_alpha_b[j], gla_norm_g[j], d)
            w_out = gla_w_out[j]
        else:
            y = _hgrn_mixer(lay, u, hgrn_w_in[j], lower_bounds[i], hgrn_norm_g[j], d)
            w_out = hgrn_w_out[j]
        kd = w_out.shape[0]
        h, u = _matmul_ln(lay, y, w_out.astype(BF16), h, mods, i, G_M, ln_mix_g[i], ln_mix_b[i], alpha,
                          next_mod=(i, SC_F, SH_F), rms_gain=rms_gain, tk=min(kd, 2048), m_rows=m_rows)

        act = _proj_conv(lay, u, w_up_all, cw_all, layer=i, tn=ffn_tn, hr=GRID_W, vertical=True,
                         lat_period=GRID_W, m_rows=m_rows, name="ffn_up_conv")
        nxt = (i + 1, SC_M, SH_M) if not last else None
        h, u = _matmul_ln(lay, act, w_down_all, h, mods, i, G_F, ln_ffn_g[i], ln_ffn_b[i], alpha,
                          next_mod=nxt, tk=tk_ffn, m_rows=m_rows, w_layer=i)
    return h.reshape(batch, seq, d)
```

```python
import functools

import jax
import jax.numpy as jnp
from jax import lax
from jax.experimental import pallas as pl
from jax.experimental.pallas import tpu as pltpu

F32 = jnp.float32
BF16 = jnp.bfloat16

EPS = 1e-5
GRID_W = 64
SSD_HEAD_DIM = 64
SSD_GROUPS = 8
SSD_STATE = 128
RET_HEADS = 8
GLA_HEADS = 4
GLA_RANK = 16
GLA_TAU = 16.0
HGRN_HEAD_K = 128
LANES = 128
SUBLANES = 8
VMEM_LIMIT = 56 * 1024 * 1024
NEG_BIG = -1e30


def _cparams(*sem):
    return pltpu.CompilerParams(dimension_semantics=sem, vmem_limit_bytes=VMEM_LIMIT)


def _sigmoid(x):
    return 1.0 / (1.0 + jnp.exp(-x))


def _silu(x):
    return x * _sigmoid(x)


def _softplus(x):
    return jnp.maximum(x, 0.0) + jnp.log(1.0 + jnp.exp(-jnp.abs(x)))


def _log_sigmoid(x):
    return jnp.minimum(x, 0.0) - jnp.log(1.0 + jnp.exp(-jnp.abs(x)))


def _pow2_tile(limit, *sizes):
    t = 1
    while t * 2 <= limit and all(s % (t * 2) == 0 for s in sizes):
        t *= 2
    return t


def _dot(a, b):
    return jnp.dot(a, b, preferred_element_type=F32)


def _dot_nt(a, b):
    return lax.dot_general(a, b, (((1,), (1,)), ((), ())), preferred_element_type=F32)


def _dot_tn(a, b):
    return lax.dot_general(a, b, (((0,), (0,)), ((), ())), preferred_element_type=F32)


class _Layout:
    def __init__(self, batch, seq, ctx):
        self.batch, self.seq, self.ctx = batch, seq, ctx
        self.m_lat = batch * seq
        self.m_ctx = batch * ctx
        self.m = self.m_lat + self.m_ctx
        self.tm = _pow2_tile(1024, seq, self.m_ctx)
        self.tm_ln = _pow2_tile(512, seq, self.m_ctx)

    def mod_row(self, tile, tm):
        start = tile * tm
        return jnp.where(start < self.m_lat, start // self.seq, self.batch)


def _mod_kernel(c_ref, w_ref, b_ref, o_ref):
    c = _silu(c_ref[...]).astype(BF16)
    o_ref[...] = _dot(c, w_ref[...].astype(BF16)) + b_ref[...]


def _modulation(c_rows, mod_w, mod_b):
    depth, d, n = mod_w.shape
    rows = c_rows.shape[0]
    tn = _pow2_tile(1024, n)
    return pl.pallas_call(
        _mod_kernel,
        grid=(depth, n // tn),
        in_specs=[pl.BlockSpec((rows, d), lambda l, j: (0, 0)),
                  pl.BlockSpec((None, d, tn), lambda l, j: (l, 0, j)),
                  pl.BlockSpec((None, 1, tn), lambda l, j: (l, 0, j))],
        out_specs=pl.BlockSpec((None, rows, tn), lambda l, j: (l, 0, j)),
        out_shape=jax.ShapeDtypeStruct((depth, rows, n), F32),
        compiler_params=_cparams("parallel", "parallel"),
        name="modulation",
    )(c_rows, mod_w, mod_b.reshape(depth, 1, n))


def _cast_pad_kernel(x_ref, o_ref):
    r, c = x_ref.shape
    ro, co = o_ref.shape
    o_ref[0:r, 0:c] = x_ref[...].astype(o_ref.dtype)
    if co > c:
        o_ref[0:r, c:co] = jnp.zeros((r, co - c), o_ref.dtype)
    if ro > r:
        o_ref[r:ro, :] = jnp.zeros((ro - r, co), o_ref.dtype)


def _cast_pad(w, in_block, out_block):
    depth, rows, cols = w.shape
    gr, gc = rows // in_block[0], cols // in_block[1]
    return pl.pallas_call(
        _cast_pad_kernel,
        grid=(depth, gr, gc),
        in_specs=[pl.BlockSpec((None,) + in_block, lambda l, i, j: (l, i, j))],
        out_specs=pl.BlockSpec((None,) + out_block, lambda l, i, j: (l, i, j)),
        out_shape=jax.ShapeDtypeStruct((depth, gr * out_block[0], gc * out_block[1]), BF16),
        compiler_params=_cparams("parallel", "parallel", "parallel"),
        name="weight_cast_pad",
    )(w)


def _modulate_kernel(x_ref, c_ref, sc_ref, sh_ref, h_ref, u_ref, *, n_lat_tiles):
    def emit(src):
        h = src[...]
        h_ref[...] = h
        u_ref[...] = (h * (1.0 + sc_ref[...]) + sh_ref[...]).astype(BF16)

    pl.when(pl.program_id(0) < n_lat_tiles)(lambda: emit(x_ref))
    pl.when(pl.program_id(0) >= n_lat_tiles)(lambda: emit(c_ref))


def _modulate(lay, x2, c2, mods, layer, sc_blk, sh_blk):
    d = x2.shape[1]
    tm = lay.tm_ln
    nl = lay.m_lat // tm
    nc = lay.m_ctx // tm
    mod_spec = lambda blk: pl.BlockSpec((None, None, 1, d), lambda i: (layer, lay.mod_row(i, tm), 0, blk))
    tok = pl.BlockSpec((tm, d), lambda i: (i, 0))
    return pl.pallas_call(
        functools.partial(_modulate_kernel, n_lat_tiles=nl),
        grid=(nl + nc,),
        in_specs=[pl.BlockSpec((tm, d), lambda i: (jnp.minimum(i, nl - 1), 0)),
                  pl.BlockSpec((tm, d), lambda i: (jnp.maximum(i - nl, 0), 0)),
                  mod_spec(sc_blk), mod_spec(sh_blk)],
        out_specs=[tok, tok],
        out_shape=[jax.ShapeDtypeStruct((lay.m, d), F32), jax.ShapeDtypeStruct((lay.m, d), BF16)],
        compiler_params=_cparams("parallel"),
        name="modulate",
    )(x2, c2, mods, mods)


def _mm_kernel(x_ref, w_ref, o_ref, *scratch, cast_w):
    if cast_w:
        wb_ref, = scratch

        @pl.when(pl.program_id(1) == 0)
        def _():
            wb_ref[...] = w_ref[...].astype(BF16)

        w = wb_ref[...]
    else:
        w = w_ref[...]
    o_ref[...] = _dot(x_ref[...], w).astype(o_ref.dtype)


def _matmul(lay, x, w, col0, n, out_dtype=BF16):
    m, k = x.shape
    tm = lay.tm
    tn = _pow2_tile(1024, n, col0) if col0 else _pow2_tile(1024, n)
    cast_w = w.dtype != BF16
    cb0 = col0 // tn
    return pl.pallas_call(
        functools.partial(_mm_kernel, cast_w=cast_w),
        grid=(n // tn, m // tm),
        in_specs=[pl.BlockSpec((tm, k), lambda j, i: (i, 0)),
                  pl.BlockSpec((k, tn), lambda j, i: (0, cb0 + j))],
        out_specs=pl.BlockSpec((tm, tn), lambda j, i: (i, j)),
        out_shape=jax.ShapeDtypeStruct((m, n), out_dtype),
        scratch_shapes=[pltpu.VMEM((k, tn), BF16)] if cast_w else [],
        compiler_params=_cparams("parallel", "arbitrary"),
        name="in_proj",
    )(x, w)


def _mm_ln_kernel(*refs, nk, n_tiles, kdim, alpha, rms, emit_u):
    it = iter(refs)
    x_ref, w_ref, h_ref, gate_ref, lng_ref, lnb_ref = (next(it) for _ in range(6))
    sc_ref = sh_ref = rg_ref = u_ref = ssq_ref = None
    if emit_u:
        sc_ref, sh_ref = next(it), next(it)
    if rms:
        rg_ref = next(it)
    hout_ref = next(it)
    if emit_u:
        u_ref = next(it)
    acc_refs = (next(it), next(it))
    ssq_refs = (next(it), next(it)) if rms else (None, None)
    i = pl.program_id(0)
    k = pl.program_id(1)

    def product():
        x = x_ref[...]
        ssq = None
        if rms:
            xf = x.astype(F32)
            ssq = jnp.sum(xf * xf, axis=1, keepdims=True)
            x = (xf * rg_ref[...]).astype(BF16)
        return _dot(x, w_ref[...]), ssq

    @pl.when(jnp.logical_and(i == 0, k == 0))
    def _():
        acc_refs[1][...] = jnp.zeros_like(acc_refs[1])
        if rms:
            ssq_refs[1][...] = jnp.zeros_like(ssq_refs[1])

    for slot in (0, 1):
        acc_ref, ssq_ref = acc_refs[slot], ssq_refs[slot]
        prev_acc, prev_ssq = acc_refs[1 - slot], ssq_refs[1 - slot]

        @pl.when(jnp.logical_and(k == 0, i % 2 == slot))
        def _(acc_ref=acc_ref, ssq_ref=ssq_ref, prev_acc=prev_acc, prev_ssq=prev_ssq):
            o_new, ssq_new = product()
            acc_ref[...] = o_new
            if rms:
                ssq_ref[...] = ssq_new
            o = prev_acc[...]
            if rms:
                o = o * lax.rsqrt(prev_ssq[...] * (1.0 / kdim) + EPS)
            y = alpha * h_ref[...] + gate_ref[...] * o
            mu = jnp.mean(y, axis=1, keepdims=True)
            yc = y - mu
            var = jnp.mean(yc * yc, axis=1, keepdims=True)
            hn = yc * lax.rsqrt(var + EPS) * lng_ref[...] + lnb_ref[...]
            hout_ref[...] = hn
            if emit_u:
                u_ref[...] = (hn * (1.0 + sc_ref[...]) + sh_ref[...]).astype(BF16)

        if nk > 1:
            @pl.when(jnp.logical_and(jnp.logical_and(k > 0, i < n_tiles), i % 2 == slot))
            def _(acc_ref=acc_ref, ssq_ref=ssq_ref):
                o_new, ssq_new = product()
                acc_ref[...] += o_new
                if rms:
                    ssq_ref[...] += ssq_new


def _matmul_ln(lay, x, w, h, mods, gate_layer, gate_blk, ln_g, ln_b, alpha, *,
               next_mod=None, rms_gain=None, tk, m_rows=None, w_layer=None):
    kdim = x.shape[1]
    m = x.shape[0] if m_rows is None else m_rows
    d = w.shape[-1]
    tm = lay.tm_ln
    nk = kdim // tk
    n_tiles = m // tm
    emit_u = next_mod is not None
    rms = rms_gain is not None
    cur = lambda i: jnp.minimum(i, n_tiles - 1)
    prev = lambda i: jnp.maximum(i - 1, 0)

    def mod_spec(layer, blk):
        return pl.BlockSpec((None, None, 1, d), lambda i, k: (layer, lay.mod_row(prev(i), tm), 0, blk))

    row_spec = pl.BlockSpec((1, d), lambda i, k: (0, 0))
    w_spec = (pl.BlockSpec((tk, d), lambda i, k: (k, 0)) if w_layer is None else
              pl.BlockSpec((None, tk, d), lambda i, k: (w_layer, k, 0)))
    in_specs = [pl.BlockSpec((tm, tk), lambda i, k: (cur(i), k)),
                w_spec,
                pl.BlockSpec((tm, d), lambda i, k: (prev(i), 0)),
                mod_spec(gate_layer, gate_blk), row_spec, row_spec]
    args = [x, w, h, mods, ln_g.reshape(1, d), ln_b.reshape(1, d)]
    if emit_u:
        nl, sc_blk, sh_blk = next_mod
        in_specs += [mod_spec(nl, sc_blk), mod_spec(nl, sh_blk)]
        args += [mods, mods]
    if rms:
        in_specs.append(pl.BlockSpec((1, tk), lambda i, k: (0, k)))
        args.append(rms_gain.reshape(1, kdim))
    out_specs = [pl.BlockSpec((tm, d), lambda i, k: (prev(i), 0))]
    out_shape = [jax.ShapeDtypeStruct((m, d), F32)]
    if emit_u:
        out_specs.append(pl.BlockSpec((tm, d), lambda i, k: (prev(i), 0)))
        out_shape.append(jax.ShapeDtypeStruct((m, d), BF16))
    scratch = [pltpu.VMEM((tm, d), F32), pltpu.VMEM((tm, d), F32)]
    if rms:
        scratch += [pltpu.VMEM((tm, 1), F32), pltpu.VMEM((tm, 1), F32)]
    outs = pl.pallas_call(
        functools.partial(_mm_ln_kernel, nk=nk, n_tiles=n_tiles, kdim=kdim, alpha=alpha, rms=rms,
                          emit_u=emit_u),
        grid=(n_tiles + 1, nk),
        in_specs=in_specs, out_specs=out_specs, out_shape=out_shape, scratch_shapes=scratch,
        compiler_params=_cparams("arbitrary", "arbitrary"),
        name="out_proj_ln",
    )(*args)
    return (outs[0], outs[1]) if emit_u else (outs[0], None)


_EXT_PAD = SUBLANES


def _proj_conv_kernel(*refs, tm, hr, vertical, second, n_split, rb, tiles_per_img, n_lat_tiles,
                      lat_period, ctx_len):
    it = iter(refs)
    xp_ref, x_ref, xn_ref, wg_ref = (next(it) for _ in range(4))
    wu_ref = next(it) if second else None
    cw_ref, o_ref = next(it), next(it)
    o2_ref = next(it) if second == "out" else None
    ext_refs = [next(it) for _ in range(n_split)]
    up_ref = next(it) if second == "mul" else None
    has_up = second == "mul"
    i = pl.program_id(1)
    is_lat = i < n_lat_tiles
    period = jnp.where(is_lat, lat_period, ctx_len)
    tn = o_ref.shape[1]
    cc = tn // n_split
    base = _EXT_PAD + hr
    vert = is_lat.astype(F32)
    tap_rows = (0, 1, 2) if vertical else (1,)
    if vertical:
        t_img = i % tiles_per_img
        has_prev = jnp.logical_and(is_lat, t_img > 0).astype(F32)
        has_next = jnp.logical_and(is_lat, t_img < tiles_per_img - 1).astype(F32)

    for sp, ext_ref in enumerate(ext_refs):
        wg = wg_ref[:, sp * cc:(sp + 1) * cc]
        prev = _dot(xp_ref[...], wg)
        nxt = _dot(xn_ref[...], wg)
        if vertical:
            prev = prev * has_prev
            nxt = nxt * has_next
        zpad = jnp.zeros((_EXT_PAD, cc), F32)
        ext_ref[0:_EXT_PAD, :] = zpad
        ext_ref[_EXT_PAD:base, :] = prev
        ext_ref[base:base + tm, :] = _dot(x_ref[...], wg)
        ext_ref[base + tm:base + tm + hr, :] = nxt
        ext_ref[base + tm + hr:base + tm + hr + _EXT_PAD, :] = zpad
    if has_up:
        up_ref[...] = _dot(x_ref[...], wu_ref[...])
    elif second == "out":
        o2_ref[...] = _dot(x_ref[...], wu_ref[...]).astype(o2_ref.dtype)

    hb = _EXT_PAD
    for sp, ext_ref in enumerate(ext_refs):
        cs = slice(sp * cc, (sp + 1) * cc)
        cw = cw_ref[:, cs]
        taps = {a: [cw[3 * a + b:3 * a + b + 1, :] * (1.0 if a == 1 else vert) for b in range(3)]
                for a in tap_rows}
        bias = cw[9:10, :]
        for blk in range(tm // rb):
            r0 = blk * rb
            col = (lax.broadcasted_iota(jnp.int32, (rb, cc), 0) + (i * tm + r0)) & (period - 1)
            s_left = s_mid = s_right = None
            for a in tap_rows:
                lo = base + r0 + (a - 1) * hr - hb
                e = ext_ref[lo:lo + rb + 2 * hb, :]
                l, c, r = taps[a][0] * e, taps[a][1] * e[hb:hb + rb], taps[a][2] * e
                s_left = l if s_left is None else s_left + l
                s_mid = c if s_mid is None else s_mid + c
                s_right = r if s_right is None else s_right + r
            left = pltpu.roll(s_left, 1, 0)[hb:hb + rb]
            right = pltpu.roll(s_right, rb + 2 * hb - 1, 0)[hb:hb + rb]
            acc = ((s_mid + bias) + jnp.where(col == 0, 0.0, left)
                   + jnp.where(col == period - 1, 0.0, right))
            act = _silu(acc)
            if has_up:
                act = act * up_ref[r0:r0 + rb, cs]
            o_ref[r0:r0 + rb, cs] = act.astype(BF16)


def _proj_conv(lay, u, w, cw, *, layer=None, w_side=None, tn, hr, vertical, lat_period, m_rows=None,
               name):
    d = u.shape[1]
    m = u.shape[0] if m_rows is None else m_rows
    n = w.shape[-1]
    tm = lay.tm
    hpt = tm // hr
    last_halo = u.shape[0] // hr - 1
    second = "mul" if layer is not None else ("out" if w_side is not None else None)
    if second == "mul":
        n //= 2
    steps = n // tn
    tn2 = tn if second == "mul" else (w_side.shape[1] // steps if second == "out" else 0)
    n_split = 1
    kern = functools.partial(_proj_conv_kernel, tm=tm, hr=hr, vertical=vertical, second=second,
                             n_split=n_split,
                             rb=min(tm, 128), tiles_per_img=lay.seq // tm,
                             n_lat_tiles=lay.m_lat // tm, lat_period=lat_period, ctx_len=lay.ctx)
    in_specs = [pl.BlockSpec((hr, d), lambda j, i: (jnp.maximum(i * hpt - 1, 0), 0)),
                pl.BlockSpec((tm, d), lambda j, i: (i, 0)),
                pl.BlockSpec((hr, d), lambda j, i: (jnp.minimum((i + 1) * hpt, last_halo), 0))]
    args = [u, u, u]
    tile = lambda width: pl.BlockSpec((tm, width), lambda j, i: (i, j))
    out_specs = [tile(tn)]
    out_shape = [jax.ShapeDtypeStruct((m, n), BF16)]
    scratch = [pltpu.VMEM((tm + 2 * hr + 2 * _EXT_PAD, tn // n_split), F32) for _ in range(n_split)]
    if second == "mul":
        in_specs += [pl.BlockSpec((None, d, tn), lambda j, i: (layer, 0, j)),
                     pl.BlockSpec((None, d, tn), lambda j, i: (layer, 0, steps + j)),
                     pl.BlockSpec((None, 16, tn), lambda j, i: (layer, 0, j))]
        args += [w, w, cw]
        scratch.append(pltpu.VMEM((tm, tn), F32))
    else:
        in_specs.append(pl.BlockSpec((d, tn), lambda j, i: (0, j)))
        args.append(w)
        if second == "out":
            assert w_side.shape[1] == steps * tn2 and tn2 % LANES == 0
            in_specs.append(pl.BlockSpec((d, tn2), lambda j, i: (0, j)))
            args.append(w_side)
            out_specs.append(tile(tn2))
            out_shape.append(jax.ShapeDtypeStruct((m, w_side.shape[1]), BF16))
        in_specs.append(pl.BlockSpec((16, tn), lambda j, i: (0, j)))
        args.append(cw)
    outs = pl.pallas_call(
        kern,
        grid=(steps, m // tm),
        in_specs=in_specs, out_specs=out_specs, out_shape=out_shape, scratch_shapes=scratch,
        compiler_params=_cparams("parallel", "parallel"),
        name=name,
    )(*args)
    return outs if second == "out" else outs[0]


def _scan_blocks(lay, q, reverse):
    nctx, nlat = lay.ctx // q, lay.seq // q
    ctx0 = lay.m_lat // q

    def row_block(b, s):
        cj = (nctx - 1 - s) if reverse else s
        lj = (nlat - 1 - (s - nctx)) if reverse else (s - nctx)
        return jnp.where(s < nctx, ctx0 + b * nctx + cj, b * nlat + lj)

    def pos_block(b, s):
        if not reverse:
            return s
        return jnp.where(s < nctx, nctx - 1 - s, nctx + nlat - 1 - (s - nctx))

    return nctx + nlat, row_block, pos_block


def _cumsum_rows(x):
    n, w = x.shape
    groups = n // SUBLANES
    x3 = x.reshape(groups, SUBLANES, w)
    sub = lax.broadcasted_iota(jnp.int32, x3.shape, 1)
    shift = 1
    while shift < SUBLANES:
        x3 = x3 + jnp.where(sub >= shift, pltpu.roll(x3, shift, 1), 0.0)
        shift *= 2
    blocks = [x3[0]]
    for g in range(1, groups):
        blocks.append(x3[g] + blocks[-1][SUBLANES - 1:SUBLANES, :])
    return jnp.concatenate(blocks, axis=0)


def _tri_mask(q, reverse):
    ii = lax.broadcasted_iota(jnp.int32, (q, q), 0)
    jj = lax.broadcasted_iota(jnp.int32, (q, q), 1)
    return (ii <= jj) if reverse else (ii >= jj)


def _ssd_scan_kernel(*refs, q, reverse, n_pairs, final):
    it = iter(refs)
    xs_refs = (next(it), next(it))
    bc_ref, dt_ref, par_ref = next(it), next(it), next(it)
    yf_ref = z_refs = dsk_ref = None
    if final:
        yf_ref = next(it)
        z_refs = (next(it), next(it))
        dsk_ref = next(it)
    o_ref = next(it)
    s_ref = next(it)

    @pl.when(pl.program_id(1) == 0)
    def _():
        s_ref[...] = jnp.zeros_like(s_ref)

    lane0 = 64 if reverse else 0
    dt = _softplus(dt_ref[...] + par_ref[0:1, :])
    la = dt * par_ref[1:2, :]
    cum = _cumsum_rows(la)
    total = cum[q - 1:q, :]
    if reverse:
        cum = total - cum + la
    cum_t = cum.T
    dt_t = dt.T
    wout_t = (jnp.exp(total - cum) * dt).T
    e_tot = jnp.exp(total)
    mask = _tri_mask(q, reverse)
    lane = lax.broadcasted_iota(jnp.int32, (1, LANES), 1)
    lo_half = lane < SSD_HEAD_DIM
    lo_sel = jnp.where(lo_half, 1.0, 0.0).astype(BF16)
    hi_sel = jnp.where(lo_half, 0.0, 1.0).astype(BF16)
    half = n_pairs // 2
    pairs_per_group = n_pairs // SSD_GROUPS

    for g in range(SSD_GROUPS):
        k = bc_ref[:, g * SSD_STATE:(g + 1) * SSD_STATE]
        qm = bc_ref[:, (SSD_GROUPS + g) * SSD_STATE:(SSD_GROUPS + g + 1) * SSD_STATE]
        scores = _dot_nt(qm, k)
        k_t = k.astype(F32).T
        for p in range(pairs_per_group):
            pair = g * pairs_per_group + p
            xr = xs_refs[pair // half]
            c0 = (pair % half) * LANES
            xs = xr[:, c0:c0 + LANES]
            xs_half = (xs * lo_sel, xs * hi_sel)
            s_pair = s_ref[pair]
            y = jnp.zeros((q, LANES), F32)
            upd = jnp.zeros((SSD_STATE, LANES), F32)
            hs = (lane0 + 2 * pair, lane0 + 2 * pair + 1)
            e_in = []
            for hh, xh in zip(hs, xs_half):
                c_col = jnp.broadcast_to(cum[:, hh:hh + 1], (q, LANES))
                e_in.append(jnp.exp(c_col))
                decay = jnp.exp(jnp.where(mask, c_col - cum_t[hh:hh + 1, :], NEG_BIG))
                wts = (scores * decay * dt_t[hh:hh + 1, :]).astype(BF16)
                y = y + _dot(wts, xh)
                upd = upd + _dot((k_t * wout_t[hh:hh + 1, :]).astype(BF16), xh)
            y = y + jnp.where(lo_half, e_in[0], e_in[1]) * _dot(qm, s_pair.astype(BF16))
            e_tot_pair = jnp.where(lo_half, e_tot[:, hs[0]:hs[0] + 1], e_tot[:, hs[1]:hs[1] + 1])
            s_ref[pair] = s_pair * e_tot_pair + upd
            oc = slice(pair * LANES, (pair + 1) * LANES)
            if final:
                y = y + yf_ref[:, oc].astype(F32)
                y = y + dsk_ref[:, oc] * xs.astype(F32)
                y = y * _silu(z_refs[pair // half][:, c0:c0 + LANES].astype(F32))
            o_ref[:, oc] = y.astype(o_ref.dtype)


def _ssd_scan(lay, xbc, zproj, dtproj, par, dskip, d_inner, reverse, yf):
    q = LANES
    m = xbc.shape[0]
    nsteps, row_block, _ = _scan_blocks(lay, q, reverse)
    hw = d_inner // 2
    assert hw == 2 * SSD_GROUPS * SSD_STATE
    n_pairs = d_inner // LANES
    final = yf is not None
    tok = lambda cb: (lambda b, s: (row_block(b, s), cb))
    in_specs = [pl.BlockSpec((q, hw), tok(0)), pl.BlockSpec((q, hw), tok(1)),
                pl.BlockSpec((q, hw), tok(2)),
                pl.BlockSpec((q, LANES), tok(0)),
                pl.BlockSpec((8, LANES), lambda b, s: (0, 0))]
    args = [xbc, xbc, xbc, dtproj, par]
    if final:
        in_specs += [pl.BlockSpec((q, d_inner), tok(0)),
                     pl.BlockSpec((q, hw), tok(0)), pl.BlockSpec((q, hw), tok(1)),
                     pl.BlockSpec((1, d_inner), lambda b, s: (0, 0))]
        args += [yf, zproj, zproj, dskip]
    kern = functools.partial(_ssd_scan_kernel, q=q, reverse=reverse, n_pairs=n_pairs, final=final)
    return pl.pallas_call(
        kern,
        grid=(lay.batch, nsteps),
        in_specs=in_specs,
        out_specs=pl.BlockSpec((q, d_inner), tok(0)),
        out_shape=jax.ShapeDtypeStruct((m, d_inner), BF16),
        scratch_shapes=[pltpu.VMEM((n_pairs, SSD_STATE, LANES), F32)],
        compiler_params=_cparams("parallel", "arbitrary"),
        name="ssd_scan_bwd" if reverse else "ssd_scan_fwd",
    )(*args)


def _ret_scan_kernel(*refs, q, reverse, hk, hv, final):
    it = iter(refs)
    q_ref, k_ref, v_ref, cos_ref, sin_ref, dl_ref = (next(it) for _ in range(6))
    yf_ref = g_ref = gng_ref = gnb_ref = None
    if final:
        yf_ref, g_ref, gng_ref, gnb_ref = (next(it) for _ in range(4))
    o_ref = next(it)
    s_ref, dec_ref, ein_ref, eout_ref = (next(it) for _ in range(4))
    half = hk // 2
    row = 1 if reverse else 0

    @pl.when(pl.program_id(1) == 0)
    def _():
        s_ref[...] = jnp.zeros_like(s_ref)

    @pl.when(pl.program_id(1) == 0)
    def _():
        ii = lax.broadcasted_iota(jnp.int32, (q, q), 0)
        jj = lax.broadcasted_iota(jnp.int32, (q, q), 1)
        dist = (ii - jj).astype(F32)
        pos = lax.broadcasted_iota(jnp.int32, (q, LANES), 0).astype(F32)
        for h in range(RET_HEADS):
            lg = _log_sigmoid(dl_ref[row:row + 1, h * LANES:(h + 1) * LANES])
            if not reverse:
                lg_rev = _log_sigmoid(dl_ref[1:2, h * LANES:(h + 1) * LANES])
                dec_ref[h] = (jnp.exp(jnp.where(ii >= jj, dist * lg[:, :q], NEG_BIG))
                              + jnp.exp(jnp.where(ii <= jj, -dist * lg_rev[:, :q], NEG_BIG)))
            if reverse:
                ein_ref[h] = jnp.exp((q - pos) * lg)
                eout_ref[h] = jnp.exp(pos * lg)
            else:
                ein_ref[h] = jnp.exp((pos + 1.0) * lg)
                eout_ref[h] = jnp.exp((q - 1.0 - pos) * lg)

    cos = cos_ref[...]
    sin = sin_ref[...]

    def rot(ref, h):
        x0 = ref[:, h * hk:h * hk + half].astype(F32)
        x1 = ref[:, h * hk + half:(h + 1) * hk].astype(F32)
        return x0 * cos - x1 * sin, x0 * sin + x1 * cos

    scale = hk ** -0.5
    for h in range(RET_HEADS):
        q0, q1 = rot(q_ref, h)
        k0, k1 = rot(k_ref, h)
        k0, k1 = k0 * scale, k1 * scale
        qr = jnp.concatenate([q0, q1], axis=1).astype(BF16)
        v = v_ref[:, h * hv:(h + 1) * hv]
        st = s_ref[h]
        ein = ein_ref[h]
        y = jnp.concatenate([ein] * (hv // LANES), axis=1) * _dot(qr, st.astype(BF16))
        if not reverse:
            kr = jnp.concatenate([k0, k1], axis=1).astype(BF16)
            y = y + _dot((_dot_nt(qr, kr) * dec_ref[h]).astype(BF16), v)
        eout = eout_ref[h]
        ks = jnp.concatenate([k0 * eout, k1 * eout], axis=1).astype(BF16)
        e_tot = ein[0:1, :] * eout[0:1, :]
        s_ref[h] = st * jnp.concatenate([e_tot] * (hv // LANES), axis=1) + _dot_tn(ks, v)
        if final:
            y = y + yf_ref[:, h * hv:(h + 1) * hv].astype(F32)
            mu = jnp.mean(y, axis=1, keepdims=True)
            yc = y - mu
            var = jnp.mean(yc * yc, axis=1, keepdims=True)
            yn = yc * lax.rsqrt(var + EPS) * gng_ref[:, h * hv:(h + 1) * hv] + gnb_ref[:, h * hv:(h + 1) * hv]
            y = _silu(g_ref[:, h * hv:(h + 1) * hv].astype(F32)) * yn
        o_ref[:, h * hv:(h + 1) * hv] = y.astype(o_ref.dtype)


def _ret_scan(lay, qk, vg, cos, sin, dl, gn_g, gn_b, qk_dim, v_dim, reverse, yf):
    q = LANES
    m = qk.shape[0]
    nsteps, row_block, pos_block = _scan_blocks(lay, q, reverse)
    final = yf is not None
    hk, hv = qk_dim // RET_HEADS, v_dim // RET_HEADS
    tok = lambda w, cb: pl.BlockSpec((q, w), lambda b, s: (row_block(b, s), cb))
    pos = pl.BlockSpec((q, hk // 2), lambda b, s: (pos_block(b, s), 0))
    const = lambda shape: pl.BlockSpec(shape, lambda b, s: (0, 0))
    in_specs = [tok(qk_dim, 0), tok(qk_dim, 1), tok(v_dim, 0), pos, pos, const(dl.shape)]
    args = [qk, qk, vg, cos, sin, dl]
    if final:
        in_specs += [tok(v_dim, 0), tok(v_dim, 1), const((1, v_dim)), const((1, v_dim))]
        args += [yf, vg, gn_g.reshape(1, v_dim), gn_b.reshape(1, v_dim)]
    kern = functools.partial(_ret_scan_kernel, q=q, reverse=reverse, hk=hk, hv=hv, final=final)
    return pl.pallas_call(
        kern,
        grid=(lay.batch, nsteps),
        in_specs=in_specs,
        out_specs=tok(v_dim, 0),
        out_shape=jax.ShapeDtypeStruct((m, v_dim), BF16),
        scratch_shapes=[pltpu.VMEM((RET_HEADS, hk, hv), F32), pltpu.VMEM((RET_HEADS, q, q), F32),
                        pltpu.VMEM((RET_HEADS, q, LANES), F32), pltpu.VMEM((RET_HEADS, q, LANES), F32)],
        compiler_params=_cparams("parallel", "arbitrary"),
        name="ret_scan_bwd" if reverse else "ret_scan_fwd",
    )(*args)


def _vscan_kernel(*refs, kind, q, reverse, heads, hk, hv, final):
    it = iter(refs)
    if kind == "gla":
        q_ref, k_ref, v_ref, a_ref, wup_ref, ab_ref = (next(it) for _ in range(6))
    else:
        q_ref, v_ref, f_ref, lb_ref = (next(it) for _ in range(4))
    yf_ref = gate_ref = ng_ref = None
    if final:
        yf_ref, gate_ref, ng_ref = (next(it) for _ in range(3))
    o_ref = next(it)
    st_ref = next(it)

    @pl.when(pl.program_id(1) == 0)
    def _():
        st_ref[...] = jnp.zeros_like(st_ref)

    mask = _tri_mask(q, reverse)
    n_sub = o_ref.shape[0] // q
    for sub in (reversed(range(n_sub)) if reverse else range(n_sub)):
        rows = slice(sub * q, (sub + 1) * q)
        if kind == "gla":
            logit = _dot(a_ref[rows, :], wup_ref[...]) + ab_ref[...]
            la = _log_sigmoid(logit) * (1.0 / GLA_TAU)
            qf = q_ref[rows, :].astype(F32) * (hk ** -0.5)
            kf = k_ref[rows, :].astype(F32)
        else:
            lb = lb_ref[...]
            f = lb + (1.0 - lb) * _sigmoid(f_ref[rows, :].astype(F32))
            la = jnp.log(f)
            kf = 1.0 - f
            qf = _silu(q_ref[rows, :].astype(F32))

        cum = _cumsum_rows(la)
        total = cum[q - 1:q, :]
        if reverse:
            cum = total - cum + la
            ref = cum[q // 2 - 1:q // 2, :]
        else:
            ref = cum[q // 2:q // 2 + 1, :]
        e_rel = jnp.exp(cum - ref)
        q_rel = qf * e_rel
        k_rel = kf * (1.0 / e_rel)
        q_abs = (q_rel * jnp.exp(ref)).astype(BF16)
        k_out = (k_rel * jnp.exp(total - ref)).astype(BF16)
        q_rel = q_rel.astype(BF16)
        k_rel = k_rel.astype(BF16)
        e_tot = jnp.exp(total)

        for h in range(heads):
            ks = slice(h * hk, (h + 1) * hk)
            vs = slice(h * hv, (h + 1) * hv)
            v = v_ref[rows, vs]
            scores = jnp.where(mask, _dot_nt(q_rel[:, ks], k_rel[:, ks]), 0.0)
            st = st_ref[h]
            y = _dot(scores.astype(BF16), v) + _dot_nt(q_abs[:, ks], st.astype(BF16))
            st_ref[h] = st * e_tot[:, ks] + _dot_tn(v, k_out[:, ks])
            if final:
                y = y + yf_ref[rows, vs].astype(F32)
                ms = jnp.mean(y * y, axis=1, keepdims=True)
                y = y * lax.rsqrt(ms + EPS) * ng_ref[...] * _silu(gate_ref[rows, vs].astype(F32))
            o_ref[rows, vs] = y.astype(o_ref.dtype)


def _vscan(lay, kind, proj, extra, norm_g, heads, hk, hv, reverse, yf):
    q = 64
    rows = 4 * q
    m = proj.shape[0]
    nsteps, row_block, _ = _scan_blocks(lay, rows, reverse)
    final = yf is not None
    kd, vd = heads * hk, heads * hv
    tok = lambda w, cb: pl.BlockSpec((rows, w), lambda b, s: (row_block(b, s), cb))
    const = lambda shape, cb: pl.BlockSpec(shape, lambda b, s: (0, cb))
    d = 1 if reverse else 0
    if kind == "gla":
        a_low, wup, ab = extra
        in_specs = [tok(kd, 0), tok(kd, 1), tok(vd, 2 * kd // vd), tok(LANES, 0),
                    const((LANES, kd), d), const((1, kd), d)]
        args = [proj, proj, proj, a_low, wup, ab]
        gate_cb = 2 * kd // vd + 1
    else:
        (lb,) = extra
        in_specs = [tok(kd, 0), tok(vd, kd // vd), tok(kd, (kd + 2 * vd) // kd + d), const((1, kd), 0)]
        args = [proj, proj, proj, lb]
        gate_cb = kd // vd + 1
    if final:
        in_specs += [tok(vd, 0), tok(vd, gate_cb), const((1, hv), 0)]
        args += [yf, proj, norm_g.reshape(1, hv)]
    kern = functools.partial(_vscan_kernel, kind=kind, q=q, reverse=reverse, heads=heads, hk=hk, hv=hv,
                             final=final)
    return pl.pallas_call(
        kern,
        grid=(lay.batch, nsteps),
        in_specs=in_specs,
        out_specs=tok(vd, 0),
        out_shape=jax.ShapeDtypeStruct((m, vd), BF16),
        scratch_shapes=[pltpu.VMEM((heads, hv, hk), F32)],
        compiler_params=_cparams("parallel", "arbitrary"),
        name=f"{kind}_scan_{'bwd' if reverse else 'fwd'}",
    )(*args)


def _lower_bound_kernel(x_ref, o_ref):
    x = x_ref[...]
    mx = jnp.max(x, axis=0, keepdims=True)
    e = jnp.exp(x - mx)
    p = e / jnp.sum(e, axis=0, keepdims=True)
    acc = jnp.zeros_like(p[0:1])
    for i in range(x.shape[0]):
        acc = acc + p[i:i + 1]
        o_ref[i:i + 1, :] = acc - p[0:1]


def _lower_bounds(logits):
    return pl.pallas_call(_lower_bound_kernel, out_shape=jax.ShapeDtypeStruct(logits.shape, F32),
                          name="hgrn_lower_bounds")(logits)


def _conv_table(taps, bias, rows):
    c = bias.shape[0]
    tab = jnp.zeros((16, c), F32)
    tab = tab.at[jnp.asarray(rows)].set(taps).at[9].set(bias)
    return tab


def _ssd_mixer(lay, u, w_in, conv_w, conv_b, dt_bias, a_log, d_skip, d_model):
    d_inner = 2 * d_model
    heads = d_inner // SSD_HEAD_DIM
    conv_ch = d_inner + 2 * SSD_GROUPS * SSD_STATE
    cw = _conv_table(conv_w, conv_b, (3, 4, 5))
    xbc = _proj_conv(lay, u, w_in[:, d_inner:d_inner + conv_ch].astype(BF16), cw, tn=1024,
                     hr=2 * SUBLANES, vertical=False, lat_period=lay.seq, name="ssd_in_proj_conv")
    zproj = _matmul(lay, u, w_in, 0, d_inner)
    dtproj = _matmul(lay, u, w_in, d_inner + conv_ch, 2 * heads, out_dtype=F32)
    par = jnp.concatenate([dt_bias.reshape(1, 2 * heads), -jnp.exp(a_log.astype(F32)).reshape(1, 2 * heads),
                           jnp.zeros((6, 2 * heads), F32)], axis=0)
    dskip = jnp.repeat(d_skip, SSD_HEAD_DIM).reshape(1, d_inner)
    yf = _ssd_scan(lay, xbc, zproj, dtproj, par, dskip, d_inner, False, None)
    return _ssd_scan(lay, xbc, zproj, dtproj, par, dskip, d_inner, True, yf)


def _deinterleave_kernel(w_ref, o_ref):
    n = w_ref.shape[1]
    src = lax.broadcasted_iota(jnp.int32, (n, n), 0)
    dst = lax.broadcasted_iota(jnp.int32, (n, n), 1)
    want = jnp.where(dst < n // 2, 2 * dst, 2 * (dst - n // 2) + 1)
    perm = jnp.where(src == want, 1.0, 0.0).astype(BF16)
    o_ref[...] = _dot(w_ref[...].astype(BF16), perm).astype(BF16)


def _deinterleave_heads(w, heads, hk):
    d = w.shape[0]
    return pl.pallas_call(
        _deinterleave_kernel,
        grid=(heads,),
        in_specs=[pl.BlockSpec((d, hk), lambda h: (0, h))],
        out_specs=pl.BlockSpec((d, hk), lambda h: (0, h)),
        out_shape=jax.ShapeDtypeStruct((d, heads * hk), BF16),
        compiler_params=_cparams("parallel"),
        name="ret_deinterleave",
    )(w)


def _ret_mixer(lay, u, w_in, decay_logit, gn_g, gn_b, d_model):
    qk, vd = d_model, 2 * d_model
    hk = qk // RET_HEADS
    w_qk = _deinterleave_heads(w_in, 2 * RET_HEADS, hk)
    qkp = _matmul(lay, u, w_qk, 0, 2 * qk)
    vgp = _matmul(lay, u, w_in, 2 * qk, 2 * vd)
    half = hk // 2
    inv_freq = 1.0 / (10000.0 ** jnp.linspace(0.0, 1.0, half, dtype=F32))
    ang = jnp.arange(lay.ctx + lay.seq, dtype=F32)[:, None] * inv_freq[None, :]
    cos, sin = jnp.cos(ang), jnp.sin(ang)
    dl = jnp.repeat(decay_logit.astype(F32), LANES, axis=1)
    dl = jnp.concatenate([dl, jnp.zeros((6, dl.shape[1]), F32)], axis=0)
    yf = _ret_scan(lay, qkp, vgp, cos, sin, dl, gn_g, gn_b, qk, vd, False, None)
    return _ret_scan(lay, qkp, vgp, cos, sin, dl, gn_g, gn_b, qk, vd, True, yf)


def _gla_mixer(lay, u, w_in, w_alpha_up, alpha_b, norm_g, d_model):
    kd, vd = d_model // 2, d_model
    n_main = 2 * kd + 2 * vd
    proj = _matmul(lay, u, w_in, 0, n_main)
    w_low = jnp.pad(w_in[:, n_main:], ((0, 0), (0, LANES - 2 * GLA_RANK))).astype(BF16)
    a_low = _matmul(lay, u, w_low, 0, LANES)
    wup = jnp.zeros((LANES, 2 * kd), F32)
    wup = wup.at[0:GLA_RANK, 0:kd].set(w_alpha_up[0]).at[GLA_RANK:2 * GLA_RANK, kd:].set(w_alpha_up[1])
    ab = alpha_b.reshape(1, 2 * kd)
    args = (lay, "gla", proj, (a_low, wup.astype(BF16), ab), norm_g, GLA_HEADS, kd // GLA_HEADS,
            vd // GLA_HEADS)
    yf = _vscan(*args, False, None)
    return _vscan(*args, True, yf)


def _hgrn_mixer(lay, u, w_in, lower_bound, norm_g, d_model):
    heads = d_model // HGRN_HEAD_K
    proj = _matmul(lay, u, w_in, 0, w_in.shape[1])
    args = (lay, "hgrn", proj, (lower_bound.reshape(1, d_model),), norm_g, heads, HGRN_HEAD_K,
            d_model // heads)
    yf = _vscan(*args, False, None)
    return _vscan(*args, True, yf)


def kernel(x, c, ctx, c_ctx, mod_w, mod_b, ln_mix_g, ln_mix_b, ln_ffn_g, ln_ffn_b, ffn_w_up, ffn_conv_w, ffn_conv_b, ffn_w_down, hgrn_lb_logits, ssd_w_in, ssd_conv_w, ssd_conv_b, ssd_dt_bias, ssd_a_log, ssd_d, ssd_norm_g, ssd_w_out, ret_w_in, ret_decay_logit, ret_gn_g, ret_gn_b, ret_w_out, gla_w_in, gla_w_alpha_up, gla_alpha_b, gla_norm_g, gla_w_out, hgrn_w_in, hgrn_norm_g, hgrn_w_out):
    batch, seq, d = x.shape
    ctx_len = ctx.shape[1]
    depth = mod_w.shape[0]
    hidden = ffn_w_down.shape[1]
    lay = _Layout(batch, seq, ctx_len)
    alpha = (2.0 * depth) ** 0.25
    n_mixers = 4
    assert GRID_W & (GRID_W - 1) == 0 and ctx_len & (ctx_len - 1) == 0 and seq & (seq - 1) == 0

    c_rows = jnp.concatenate([c, c_ctx[None], jnp.zeros((SUBLANES - batch - 1, d), F32)], axis=0)
    mods = _modulation(c_rows, mod_w, mod_b).reshape(depth, SUBLANES, 1, 6 * d)
    SH_M, SC_M, G_M, SH_F, SC_F, G_F = range(6)

    lower_bounds = _lower_bounds(hgrn_lb_logits.astype(F32))

    h, u = _modulate(lay, x.reshape(lay.m_lat, d), ctx.reshape(lay.m_ctx, d), mods, 0, SC_M, SH_M)

    ffn_tn = 512
    hp = -(-hidden // ffn_tn) * ffn_tn
    tk_ffn = hp // 4 if (hp // 4) % LANES == 0 else ffn_tn
    pad = hp - hidden
    del pad
    w_up_all = _cast_pad(ffn_w_up, (d // 8, hidden), (d // 8, hp))
    w_down_all = _cast_pad(ffn_w_down, (hidden, d // 8), (hp, d // 8))
    cw_all = jnp.zeros((depth, 16, hp), F32)
    cw_all = cw_all.at[:, 0:9, :hidden].set(ffn_conv_w.reshape(depth, 9, hidden))
    cw_all = cw_all.at[:, 9, :hidden].set(ffn_conv_b)
    for i in range(depth):
        kind, j = i % n_mixers, i // n_mixers
        last = i == depth - 1
        m_rows = lay.m_lat if last else None
        rms_gain = None
        if kind == 0:
            y = _ssd_mixer(lay, u, ssd_w_in[j], ssd_conv_w[j], ssd_conv_b[j], ssd_dt_bias[j], ssd_a_log[j],
                           ssd_d[j], d)
            w_out, rms_gain = ssd_w_out[j], ssd_norm_g[j]
        elif kind == 1:
            y = _ret_mixer(lay, u, ret_w_in[j], ret_decay_logit[j], ret_gn_g[j], ret_gn_b[j], d)
            w_out = ret_w_out[j]
        elif kind == 2:
            y = _gla_mixer(lay, u, gla_w_in[j], gla_w_alpha_up[j], gla_alpha_b[j], gla_norm_g[j], d)
            w_out = gla_w_out[j]
        else:
            y = _hgrn_mixer(lay, u, hgrn_w_in[j], lower_bounds[i], hgrn_norm_g[j], d)
            w_out = hgrn_w_out[j]
        kd = w_out.shape[0]
        h, u = _matmul_ln(lay, y, w_out.astype(BF16), h, mods, i, G_M, ln_mix_g[i], ln_mix_b[i], alpha,
                          next_mod=(i, SC_F, SH_F), rms_gain=rms_gain, tk=min(kd, 2048), m_rows=m_rows)

        act = _proj_conv(lay, u, w_up_all, cw_all, layer=i, tn=ffn_tn, hr=GRID_W, vertical=True,
                         lat_period=GRID_W, m_rows=m_rows, name="ffn_up_conv")
        nxt = (i + 1, SC_M, SH_M) if not last else None
        h, u = _matmul_ln(lay, act, w_down_all, h, mods, i, G_F, ln_ffn_g[i], ln_ffn_b[i], alpha,
                          next_mod=nxt, tk=tk_ffn, m_rows=m_rows, w_layer=i)
    return h.reshape(batch, seq, d)
```

```python
import functools

import jax
import jax.numpy as jnp
from jax import lax
from jax.experimental import pallas as pl
from jax.experimental.pallas import tpu as pltpu

F32 = jnp.float32
BF16 = jnp.bfloat16

EPS = 1e-5
GRID_W = 64
SSD_HEAD_DIM = 64
SSD_GROUPS = 8
SSD_STATE = 128
RET_HEADS = 8
GLA_HEADS = 4
GLA_RANK = 16
GLA_TAU = 16.0
HGRN_HEAD_K = 128
LANES = 128
SUBLANES = 8
VMEM_LIMIT = 56 * 1024 * 1024
NEG_BIG = -1e30


def _cparams(*sem):
    return pltpu.CompilerParams(dimension_semantics=sem, vmem_limit_bytes=VMEM_LIMIT)


def _sigmoid(x):
    return 1.0 / (1.0 + jnp.exp(-x))


def _silu(x):
    return x * _sigmoid(x)


def _softplus(x):
    return jnp.maximum(x, 0.0) + jnp.log(1.0 + jnp.exp(-jnp.abs(x)))


def _log_sigmoid(x):
    return jnp.minimum(x, 0.0) - jnp.log(1.0 + jnp.exp(-jnp.abs(x)))


def _pow2_tile(limit, *sizes):
    t = 1
    while t * 2 <= limit and all(s % (t * 2) == 0 for s in sizes):
        t *= 2
    return t


def _dot(a, b):
    return jnp.dot(a, b, preferred_element_type=F32)


def _dot_nt(a, b):
    return lax.dot_general(a, b, (((1,), (1,)), ((), ())), preferred_element_type=F32)


def _dot_tn(a, b):
    return lax.dot_general(a, b, (((0,), (0,)), ((), ())), preferred_element_type=F32)


class _Layout:
    def __init__(self, batch, seq, ctx):
        self.batch, self.seq, self.ctx = batch, seq, ctx
        self.m_lat = batch * seq
        self.m_ctx = batch * ctx
        self.m = self.m_lat + self.m_ctx
        self.tm = _pow2_tile(1024, seq, self.m_ctx)
        self.tm_ln = _pow2_tile(512, seq, self.m_ctx)

    def mod_row(self, tile, tm):
        start = tile * tm
        return jnp.where(start < self.m_lat, start // self.seq, self.batch)


def _mod_kernel(c_ref, w_ref, b_ref, o_ref):
    c = _silu(c_ref[...]).astype(BF16)
    o_ref[...] = _dot(c, w_ref[...].astype(BF16)) + b_ref[...]


def _modulation(c_rows, mod_w, mod_b):
    depth, d, n = mod_w.shape
    rows = c_rows.shape[0]
    tn = _pow2_tile(1024, n)
    return pl.pallas_call(
        _mod_kernel,
        grid=(depth, n // tn),
        in_specs=[pl.BlockSpec((rows, d), lambda l, j: (0, 0)),
                  pl.BlockSpec((None, d, tn), lambda l, j: (l, 0, j)),
                  pl.BlockSpec((None, 1, tn), lambda l, j: (l, 0, j))],
        out_specs=pl.BlockSpec((None, rows, tn), lambda l, j: (l, 0, j)),
        out_shape=jax.ShapeDtypeStruct((depth, rows, n), F32),
        compiler_params=_cparams("parallel", "parallel"),
        name="modulation",
    )(c_rows, mod_w, mod_b.reshape(depth, 1, n))


def _cast_pad_kernel(x_ref, o_ref):
    r, c = x_ref.shape
    ro, co = o_ref.shape
    o_ref[0:r, 0:c] = x_ref[...].astype(o_ref.dtype)
    if co > c:
        o_ref[0:r, c:co] = jnp.zeros((r, co - c), o_ref.dtype)
    if ro > r:
        o_ref[r:ro, :] = jnp.zeros((ro - r, co), o_ref.dtype)


def _cast_pad(w, in_block, out_block, col_block0=0, col_blocks=None):
    depth, rows, cols = w.shape
    gr = rows // in_block[0]
    gc = cols // in_block[1] if col_blocks is None else col_blocks
    return pl.pallas_call(
        _cast_pad_kernel,
        grid=(depth, gr, gc),
        in_specs=[pl.BlockSpec((None,) + in_block, lambda l, i, j: (l, i, col_block0 + j))],
        out_specs=pl.BlockSpec((None,) + out_block, lambda l, i, j: (l, i, j)),
        out_shape=jax.ShapeDtypeStruct((depth, gr * out_block[0], gc * out_block[1]), BF16),
        compiler_params=_cparams("parallel", "parallel", "parallel"),
        name="weight_cast_pad",
    )(w)


def _modulate_kernel(x_ref, c_ref, sc_ref, sh_ref, h_ref, u_ref, *, n_lat_tiles):
    def emit(src):
        h = src[...]
        h_ref[...] = h
        u_ref[...] = (h * (1.0 + sc_ref[...]) + sh_ref[...]).astype(BF16)

    pl.when(pl.program_id(0) < n_lat_tiles)(lambda: emit(x_ref))
    pl.when(pl.program_id(0) >= n_lat_tiles)(lambda: emit(c_ref))


def _modulate(lay, x2, c2, mods, layer, sc_blk, sh_blk):
    d = x2.shape[1]
    tm = lay.tm_ln
    nl = lay.m_lat // tm
    nc = lay.m_ctx // tm
    mod_spec = lambda blk: pl.BlockSpec((None, None, 1, d), lambda i: (layer, lay.mod_row(i, tm), 0, blk))
    tok = pl.BlockSpec((tm, d), lambda i: (i, 0))
    return pl.pallas_call(
        functools.partial(_modulate_kernel, n_lat_tiles=nl),
        grid=(nl + nc,),
        in_specs=[pl.BlockSpec((tm, d), lambda i: (jnp.minimum(i, nl - 1), 0)),
                  pl.BlockSpec((tm, d), lambda i: (jnp.maximum(i - nl, 0), 0)),
                  mod_spec(sc_blk), mod_spec(sh_blk)],
        out_specs=[tok, tok],
        out_shape=[jax.ShapeDtypeStruct((lay.m, d), F32), jax.ShapeDtypeStruct((lay.m, d), BF16)],
        compiler_params=_cparams("parallel"),
        name="modulate",
    )(x2, c2, mods, mods)


def _mm_kernel(x_ref, w_ref, o_ref, *scratch, cast_w):
    if cast_w:
        wb_ref, = scratch

        @pl.when(pl.program_id(1) == 0)
        def _():
            wb_ref[...] = w_ref[...].astype(BF16)

        w = wb_ref[...]
    else:
        w = w_ref[...]
    o_ref[...] = _dot(x_ref[...], w).astype(o_ref.dtype)


def _matmul(lay, x, w, col0, n, out_dtype=BF16):
    m, k = x.shape
    tm = lay.tm
    tn = _pow2_tile(1024, n, col0) if col0 else _pow2_tile(1024, n)
    cast_w = w.dtype != BF16
    cb0 = col0 // tn
    return pl.pallas_call(
        functools.partial(_mm_kernel, cast_w=cast_w),
        grid=(n // tn, m // tm),
        in_specs=[pl.BlockSpec((tm, k), lambda j, i: (i, 0)),
                  pl.BlockSpec((k, tn), lambda j, i: (0, cb0 + j))],
        out_specs=pl.BlockSpec((tm, tn), lambda j, i: (i, j)),
        out_shape=jax.ShapeDtypeStruct((m, n), out_dtype),
        scratch_shapes=[pltpu.VMEM((k, tn), BF16)] if cast_w else [],
        compiler_params=_cparams("parallel", "arbitrary"),
        name="in_proj",
    )(x, w)


def _mm_ln_kernel(*refs, nk, n_tiles, kdim, alpha, rms, emit_u):
    it = iter(refs)
    x_ref, w_ref, h_ref, gate_ref, lng_ref, lnb_ref = (next(it) for _ in range(6))
    sc_ref = sh_ref = rg_ref = u_ref = ssq_ref = None
    if emit_u:
        sc_ref, sh_ref = next(it), next(it)
    if rms:
        rg_ref = next(it)
    hout_ref = next(it)
    if emit_u:
        u_ref = next(it)
    acc_refs = (next(it), next(it))
    ssq_refs = (next(it), next(it)) if rms else (None, None)
    i = pl.program_id(0)
    k = pl.program_id(1)

    def product():
        x = x_ref[...]
        ssq = None
        if rms:
            xf = x.astype(F32)
            ssq = jnp.sum(xf * xf, axis=1, keepdims=True)
            x = (xf * rg_ref[...]).astype(BF16)
        return _dot(x, w_ref[...]), ssq

    @pl.when(jnp.logical_and(i == 0, k == 0))
    def _():
        acc_refs[1][...] = jnp.zeros_like(acc_refs[1])
        if rms:
            ssq_refs[1][...] = jnp.zeros_like(ssq_refs[1])

    for slot in (0, 1):
        acc_ref, ssq_ref = acc_refs[slot], ssq_refs[slot]
        prev_acc, prev_ssq = acc_refs[1 - slot], ssq_refs[1 - slot]

        @pl.when(jnp.logical_and(k == 0, i % 2 == slot))
        def _(acc_ref=acc_ref, ssq_ref=ssq_ref, prev_acc=prev_acc, prev_ssq=prev_ssq):
            o_new, ssq_new = product()
            acc_ref[...] = o_new
            if rms:
                ssq_ref[...] = ssq_new
            o = prev_acc[...]
            if rms:
                o = o * lax.rsqrt(prev_ssq[...] * (1.0 / kdim) + EPS)
            y = alpha * h_ref[...] + gate_ref[...] * o
            mu = jnp.mean(y, axis=1, keepdims=True)
            yc = y - mu
            var = jnp.mean(yc * yc, axis=1, keepdims=True)
            hn = yc * lax.rsqrt(var + EPS) * lng_ref[...] + lnb_ref[...]
            hout_ref[...] = hn
            if emit_u:
                u_ref[...] = (hn * (1.0 + sc_ref[...]) + sh_ref[...]).astype(BF16)

        if nk > 1:
            @pl.when(jnp.logical_and(jnp.logical_and(k > 0, i < n_tiles), i % 2 == slot))
            def _(acc_ref=acc_ref, ssq_ref=ssq_ref):
                o_new, ssq_new = product()
                acc_ref[...] += o_new
                if rms:
                    ssq_ref[...] += ssq_new


def _matmul_ln(lay, x, w, h, mods, gate_layer, gate_blk, ln_g, ln_b, alpha, *,
               next_mod=None, rms_gain=None, tk, m_rows=None, w_layer=None):
    kdim = x.shape[1]
    m = x.shape[0] if m_rows is None else m_rows
    d = w.shape[-1]
    tm = lay.tm_ln
    nk = kdim // tk
    n_tiles = m // tm
    emit_u = next_mod is not None
    rms = rms_gain is not None
    cur = lambda i: jnp.minimum(i, n_tiles - 1)
    prev = lambda i: jnp.maximum(i - 1, 0)

    def mod_spec(layer, blk):
        return pl.BlockSpec((None, None, 1, d), lambda i, k: (layer, lay.mod_row(prev(i), tm), 0, blk))

    row_spec = pl.BlockSpec((1, d), lambda i, k: (0, 0))
    w_spec = (pl.BlockSpec((tk, d), lambda i, k: (k, 0)) if w_layer is None else
              pl.BlockSpec((None, tk, d), lambda i, k: (w_layer, k, 0)))
    in_specs = [pl.BlockSpec((tm, tk), lambda i, k: (cur(i), k)),
                w_spec,
                pl.BlockSpec((tm, d), lambda i, k: (prev(i), 0)),
                mod_spec(gate_layer, gate_blk), row_spec, row_spec]
    args = [x, w, h, mods, ln_g.reshape(1, d), ln_b.reshape(1, d)]
    if emit_u:
        nl, sc_blk, sh_blk = next_mod
        in_specs += [mod_spec(nl, sc_blk), mod_spec(nl, sh_blk)]
        args += [mods, mods]
    if rms:
        in_specs.append(pl.BlockSpec((1, tk), lambda i, k: (0, k)))
        args.append(rms_gain.reshape(1, kdim))
    out_specs = [pl.BlockSpec((tm, d), lambda i, k: (prev(i), 0))]
    out_shape = [jax.ShapeDtypeStruct((m, d), F32)]
    if emit_u:
        out_specs.append(pl.BlockSpec((tm, d), lambda i, k: (prev(i), 0)))
        out_shape.append(jax.ShapeDtypeStruct((m, d), BF16))
    scratch = [pltpu.VMEM((tm, d), F32), pltpu.VMEM((tm, d), F32)]
    if rms:
        scratch += [pltpu.VMEM((tm, 1), F32), pltpu.VMEM((tm, 1), F32)]
    outs = pl.pallas_call(
        functools.partial(_mm_ln_kernel, nk=nk, n_tiles=n_tiles, kdim=kdim, alpha=alpha, rms=rms,
                          emit_u=emit_u),
        grid=(n_tiles + 1, nk),
        in_specs=in_specs, out_specs=out_specs, out_shape=out_shape, scratch_shapes=scratch,
        compiler_params=_cparams("arbitrary", "arbitrary"),
        name="out_proj_ln",
    )(*args)
    return (outs[0], outs[1]) if emit_u else (outs[0], None)


_EXT_PAD = SUBLANES


def _proj_conv_kernel(*refs, tm, hr, vertical, second, n_split, rb, tiles_per_img, n_lat_tiles,
                      lat_period, ctx_len):
    it = iter(refs)
    xp_ref, x_ref, xn_ref, wg_ref = (next(it) for _ in range(4))
    wu_ref = next(it) if second else None
    cw_ref, o_ref = next(it), next(it)
    o2_ref = next(it) if second == "out" else None
    ext_refs = [next(it) for _ in range(n_split)]
    up_ref = next(it) if second == "mul" else None
    has_up = second == "mul"
    i = pl.program_id(1)
    is_lat = i < n_lat_tiles
    period = jnp.where(is_lat, lat_period, ctx_len)
    tn = o_ref.shape[1]
    cc = tn // n_split
    base = _EXT_PAD + hr
    vert = is_lat.astype(F32)
    tap_rows = (0, 1, 2) if vertical else (1,)
    if vertical:
        t_img = i % tiles_per_img
        has_prev = jnp.logical_and(is_lat, t_img > 0).astype(F32)
        has_next = jnp.logical_and(is_lat, t_img < tiles_per_img - 1).astype(F32)

    for sp, ext_ref in enumerate(ext_refs):
        wg = wg_ref[:, sp * cc:(sp + 1) * cc]
        prev = _dot(xp_ref[...], wg)
        nxt = _dot(xn_ref[...], wg)
        if vertical:
            prev = prev * has_prev
            nxt = nxt * has_next
        zpad = jnp.zeros((_EXT_PAD, cc), F32)
        ext_ref[0:_EXT_PAD, :] = zpad
        ext_ref[_EXT_PAD:base, :] = prev
        ext_ref[base:base + tm, :] = _dot(x_ref[...], wg)
        ext_ref[base + tm:base + tm + hr, :] = nxt
        ext_ref[base + tm + hr:base + tm + hr + _EXT_PAD, :] = zpad
    if has_up:
        up_ref[...] = _dot(x_ref[...], wu_ref[...])
    elif second == "out":
        o2_ref[...] = _dot(x_ref[...], wu_ref[...]).astype(o2_ref.dtype)

    hb = _EXT_PAD
    for sp, ext_ref in enumerate(ext_refs):
        cs = slice(sp * cc, (sp + 1) * cc)
        cw = cw_ref[:, cs]
        taps = {a: [cw[3 * a + b:3 * a + b + 1, :] * (1.0 if a == 1 else vert) for b in range(3)]
                for a in tap_rows}
        bias = cw[9:10, :]
        for blk in range(tm // rb):
            r0 = blk * rb
            col = (lax.broadcasted_iota(jnp.int32, (rb, cc), 0) + (i * tm + r0)) & (period - 1)
            s_left = s_mid = s_right = None
            for a in tap_rows:
                lo = base + r0 + (a - 1) * hr - hb
                e = ext_ref[lo:lo + rb + 2 * hb, :]
                l, c, r = taps[a][0] * e, taps[a][1] * e[hb:hb + rb], taps[a][2] * e
                s_left = l if s_left is None else s_left + l
                s_mid = c if s_mid is None else s_mid + c
                s_right = r if s_right is None else s_right + r
            left = pltpu.roll(s_left, 1, 0)[hb:hb + rb]
            right = pltpu.roll(s_right, rb + 2 * hb - 1, 0)[hb:hb + rb]
            acc = ((s_mid + bias) + jnp.where(col == 0, 0.0, left)
                   + jnp.where(col == period - 1, 0.0, right))
            act = _silu(acc)
            if has_up:
                act = act * up_ref[r0:r0 + rb, cs]
            o_ref[r0:r0 + rb, cs] = act.astype(BF16)


def _proj_conv(lay, u, w, cw, *, layer=None, w_side=None, tn, hr, vertical, lat_period, m_rows=None,
               name):
    d = u.shape[1]
    m = u.shape[0] if m_rows is None else m_rows
    n = w.shape[-1]
    tm = lay.tm
    hpt = tm // hr
    last_halo = u.shape[0] // hr - 1
    second = "mul" if layer is not None else ("out" if w_side is not None else None)
    if second == "mul":
        n //= 2
    steps = n // tn
    tn2 = tn if second == "mul" else (w_side.shape[1] // steps if second == "out" else 0)
    n_split = 1
    kern = functools.partial(_proj_conv_kernel, tm=tm, hr=hr, vertical=vertical, second=second,
                             n_split=n_split,
                             rb=min(tm, 128), tiles_per_img=lay.seq // tm,
                             n_lat_tiles=lay.m_lat // tm, lat_period=lat_period, ctx_len=lay.ctx)
    in_specs = [pl.BlockSpec((hr, d), lambda j, i: (jnp.maximum(i * hpt - 1, 0), 0)),
                pl.BlockSpec((tm, d), lambda j, i: (i, 0)),
                pl.BlockSpec((hr, d), lambda j, i: (jnp.minimum((i + 1) * hpt, last_halo), 0))]
    args = [u, u, u]
    tile = lambda width: pl.BlockSpec((tm, width), lambda j, i: (i, j))
    out_specs = [tile(tn)]
    out_shape = [jax.ShapeDtypeStruct((m, n), BF16)]
    scratch = [pltpu.VMEM((tm + 2 * hr + 2 * _EXT_PAD, tn // n_split), F32) for _ in range(n_split)]
    if second == "mul":
        in_specs += [pl.BlockSpec((None, d, tn), lambda j, i: (layer, 0, j)),
                     pl.BlockSpec((None, d, tn), lambda j, i: (layer, 0, steps + j)),
                     pl.BlockSpec((None, 16, tn), lambda j, i: (layer, 0, j))]
        args += [w, w, cw]
        scratch.append(pltpu.VMEM((tm, tn), F32))
    else:
        in_specs.append(pl.BlockSpec((d, tn), lambda j, i: (0, j)))
        args.append(w)
        if second == "out":
            assert w_side.shape[1] == steps * tn2 and tn2 % LANES == 0
            in_specs.append(pl.BlockSpec((d, tn2), lambda j, i: (0, j)))
            args.append(w_side)
            out_specs.append(tile(tn2))
            out_shape.append(jax.ShapeDtypeStruct((m, w_side.shape[1]), BF16))
        in_specs.append(pl.BlockSpec((16, tn), lambda j, i: (0, j)))
        args.append(cw)
    outs = pl.pallas_call(
        kern,
        grid=(steps, m // tm),
        in_specs=in_specs, out_specs=out_specs, out_shape=out_shape, scratch_shapes=scratch,
        compiler_params=_cparams("parallel", "parallel"),
        name=name,
    )(*args)
    return outs if second == "out" else outs[0]


def _scan_blocks(lay, q, reverse):
    nctx, nlat = lay.ctx // q, lay.seq // q
    ctx0 = lay.m_lat // q

    def row_block(b, s):
        cj = (nctx - 1 - s) if reverse else s
        lj = (nlat - 1 - (s - nctx)) if reverse else (s - nctx)
        return jnp.where(s < nctx, ctx0 + b * nctx + cj, b * nlat + lj)

    def pos_block(b, s):
        if not reverse:
            return s
        return jnp.where(s < nctx, nctx - 1 - s, nctx + nlat - 1 - (s - nctx))

    return nctx + nlat, row_block, pos_block


def _cumsum_rows(x):
    n, w = x.shape
    groups = n // SUBLANES
    x3 = x.reshape(groups, SUBLANES, w)
    sub = lax.broadcasted_iota(jnp.int32, x3.shape, 1)
    shift = 1
    while shift < SUBLANES:
        x3 = x3 + jnp.where(sub >= shift, pltpu.roll(x3, shift, 1), 0.0)
        shift *= 2
    blocks = [x3[0]]
    for g in range(1, groups):
        blocks.append(x3[g] + blocks[-1][SUBLANES - 1:SUBLANES, :])
    return jnp.concatenate(blocks, axis=0)


def _tri_mask(q, reverse):
    ii = lax.broadcasted_iota(jnp.int32, (q, q), 0)
    jj = lax.broadcasted_iota(jnp.int32, (q, q), 1)
    return (ii <= jj) if reverse else (ii >= jj)


def _ssd_scan_kernel(*refs, q, reverse, n_pairs, final):
    it = iter(refs)
    xs_refs = (next(it), next(it))
    bc_ref, dt_ref, par_ref = next(it), next(it), next(it)
    yf_ref = z_refs = dsk_ref = None
    if final:
        yf_ref = next(it)
        z_refs = (next(it), next(it))
        dsk_ref = next(it)
    o_ref = next(it)
    s_ref = next(it)

    @pl.when(pl.program_id(1) == 0)
    def _():
        s_ref[...] = jnp.zeros_like(s_ref)

    lane0 = 64 if reverse else 0
    dt = _softplus(dt_ref[...] + par_ref[0:1, :])
    la = dt * par_ref[1:2, :]
    cum = _cumsum_rows(la)
    total = cum[q - 1:q, :]
    if reverse:
        cum = total - cum + la
    cum_t = cum.T
    dt_t = dt.T
    wout_t = (jnp.exp(total - cum) * dt).T
    e_tot = jnp.exp(total)
    mask = _tri_mask(q, reverse)
    lane = lax.broadcasted_iota(jnp.int32, (1, LANES), 1)
    lo_half = lane < SSD_HEAD_DIM
    lo_sel = jnp.where(lo_half, 1.0, 0.0).astype(BF16)
    hi_sel = jnp.where(lo_half, 0.0, 1.0).astype(BF16)
    half = n_pairs // 2
    pairs_per_group = n_pairs // SSD_GROUPS

    for g in range(SSD_GROUPS):
        k = bc_ref[:, g * SSD_STATE:(g + 1) * SSD_STATE]
        qm = bc_ref[:, (SSD_GROUPS + g) * SSD_STATE:(SSD_GROUPS + g + 1) * SSD_STATE]
        scores = _dot_nt(qm, k)
        k_t = k.astype(F32).T
        for p in range(pairs_per_group):
            pair = g * pairs_per_group + p
            xr = xs_refs[pair // half]
            c0 = (pair % half) * LANES
            xs = xr[:, c0:c0 + LANES]
            xs_half = (xs * lo_sel, xs * hi_sel)
            s_pair = s_ref[pair]
            y = jnp.zeros((q, LANES), F32)
            upd = jnp.zeros((SSD_STATE, LANES), F32)
            hs = (lane0 + 2 * pair, lane0 + 2 * pair + 1)
            e_in = []
            for hh, xh in zip(hs, xs_half):
                c_col = jnp.broadcast_to(cum[:, hh:hh + 1], (q, LANES))
                e_in.append(jnp.exp(c_col))
                decay = jnp.exp(jnp.where(mask, c_col - cum_t[hh:hh + 1, :], NEG_BIG))
                wts = (scores * decay * dt_t[hh:hh + 1, :]).astype(BF16)
                y = y + _dot(wts, xh)
                upd = upd + _dot((k_t * wout_t[hh:hh + 1, :]).astype(BF16), xh)
            y = y + jnp.where(lo_half, e_in[0], e_in[1]) * _dot(qm, s_pair.astype(BF16))
            e_tot_pair = jnp.where(lo_half, e_tot[:, hs[0]:hs[0] + 1], e_tot[:, hs[1]:hs[1] + 1])
            s_ref[pair] = s_pair * e_tot_pair + upd
            oc = slice(pair * LANES, (pair + 1) * LANES)
            if final:
                y = y + yf_ref[:, oc].astype(F32)
                y = y + dsk_ref[:, oc] * xs.astype(F32)
                y = y * _silu(z_refs[pair // half][:, c0:c0 + LANES].astype(F32))
            o_ref[:, oc] = y.astype(o_ref.dtype)


def _ssd_scan(lay, xbc, zproj, dtproj, par, dskip, d_inner, reverse, yf):
    q = LANES
    m = xbc.shape[0]
    nsteps, row_block, _ = _scan_blocks(lay, q, reverse)
    hw = d_inner // 2
    assert hw == 2 * SSD_GROUPS * SSD_STATE
    n_pairs = d_inner // LANES
    final = yf is not None
    tok = lambda cb: (lambda b, s: (row_block(b, s), cb))
    in_specs = [pl.BlockSpec((q, hw), tok(0)), pl.BlockSpec((q, hw), tok(1)),
                pl.BlockSpec((q, hw), tok(2)),
                pl.BlockSpec((q, LANES), tok(0)),
                pl.BlockSpec((8, LANES), lambda b, s: (0, 0))]
    args = [xbc, xbc, xbc, dtproj, par]
    if final:
        in_specs += [pl.BlockSpec((q, d_inner), tok(0)),
                     pl.BlockSpec((q, hw), tok(0)), pl.BlockSpec((q, hw), tok(1)),
                     pl.BlockSpec((1, d_inner), lambda b, s: (0, 0))]
        args += [yf, zproj, zproj, dskip]
    kern = functools.partial(_ssd_scan_kernel, q=q, reverse=reverse, n_pairs=n_pairs, final=final)
    return pl.pallas_call(
        kern,
        grid=(lay.batch, nsteps),
        in_specs=in_specs,
        out_specs=pl.BlockSpec((q, d_inner), tok(0)),
        out_shape=jax.ShapeDtypeStruct((m, d_inner), BF16),
        scratch_shapes=[pltpu.VMEM((n_pairs, SSD_STATE, LANES), F32)],
        compiler_params=_cparams("parallel", "arbitrary"),
        name="ssd_scan_bwd" if reverse else "ssd_scan_fwd",
    )(*args)


def _ret_scan_kernel(*refs, q, reverse, hk, hv, final):
    it = iter(refs)
    q_ref, k_ref, v_ref, cos_ref, sin_ref, dl_ref = (next(it) for _ in range(6))
    yf_ref = g_ref = gng_ref = gnb_ref = None
    if final:
        yf_ref, g_ref, gng_ref, gnb_ref = (next(it) for _ in range(4))
    o_ref = next(it)
    qrot_ref, krot_ref = (None, None) if reverse else (next(it), next(it))
    s_ref, dec_ref, ein_ref, eout_ref = (next(it) for _ in range(4))
    half = hk // 2
    row = 1 if reverse else 0

    @pl.when(pl.program_id(1) == 0)
    def _():
        s_ref[...] = jnp.zeros_like(s_ref)

    @pl.when(pl.program_id(1) == 0)
    def _():
        ii = lax.broadcasted_iota(jnp.int32, (q, q), 0)
        jj = lax.broadcasted_iota(jnp.int32, (q, q), 1)
        dist = (ii - jj).astype(F32)
        pos = lax.broadcasted_iota(jnp.int32, (q, LANES), 0).astype(F32)
        for h in range(RET_HEADS):
            lg = _log_sigmoid(dl_ref[row:row + 1, h * LANES:(h + 1) * LANES])
            if not reverse:
                lg_rev = _log_sigmoid(dl_ref[1:2, h * LANES:(h + 1) * LANES])
                dec_ref[h] = (jnp.exp(jnp.where(ii >= jj, dist * lg[:, :q], NEG_BIG))
                              + jnp.exp(jnp.where(ii <= jj, -dist * lg_rev[:, :q], NEG_BIG)))
            if reverse:
                ein_ref[h] = jnp.exp((q - pos) * lg)
                eout_ref[h] = jnp.exp(pos * lg)
            else:
                ein_ref[h] = jnp.exp((pos + 1.0) * lg)
                eout_ref[h] = jnp.exp((q - 1.0 - pos) * lg)

    scale = hk ** -0.5
    n_sub = o_ref.shape[0] // q
    for sub in (reversed(range(n_sub)) if reverse else range(n_sub)):
        rows = slice(sub * q, (sub + 1) * q)
        cos = cos_ref[rows, :]
        sin = sin_ref[rows, :]

        def rot(ref, h):
            x0 = ref[rows, h * hk:h * hk + half].astype(F32)
            x1 = ref[rows, h * hk + half:(h + 1) * hk].astype(F32)
            return x0 * cos - x1 * sin, x0 * sin + x1 * cos

        for h in range(RET_HEADS):
            vs = slice(h * hv, (h + 1) * hv)
            hs = slice(h * hk, (h + 1) * hk)
            if reverse:
                qr = q_ref[rows, hs]
                kf = k_ref[rows, hs].astype(F32)
                k0, k1 = kf[:, :half], kf[:, half:]
            else:
                q0, q1 = rot(q_ref, h)
                k0, k1 = rot(k_ref, h)
                k0, k1 = k0 * scale, k1 * scale
                qr = jnp.concatenate([q0, q1], axis=1).astype(BF16)
            v = v_ref[rows, vs]
            st = s_ref[h]
            ein = ein_ref[h]
            y = jnp.concatenate([ein] * (hv // LANES), axis=1) * _dot(qr, st.astype(BF16))
            if not reverse:
                kr = jnp.concatenate([k0, k1], axis=1).astype(BF16)
                y = y + _dot((_dot_nt(qr, kr) * dec_ref[h]).astype(BF16), v)
                qrot_ref[rows, hs] = qr
                krot_ref[rows, hs] = kr
            eout = eout_ref[h]
            ks = jnp.concatenate([k0 * eout, k1 * eout], axis=1).astype(BF16)
            e_tot = ein[0:1, :] * eout[0:1, :]
            s_ref[h] = st * jnp.concatenate([e_tot] * (hv // LANES), axis=1) + _dot_tn(ks, v)
            if final:
                y = y + yf_ref[rows, vs].astype(F32)
                mu = jnp.mean(y, axis=1, keepdims=True)
                yc = y - mu
                var = jnp.mean(yc * yc, axis=1, keepdims=True)
                yn = yc * lax.rsqrt(var + EPS) * gng_ref[:, vs] + gnb_ref[:, vs]
                y = _silu(g_ref[rows, vs].astype(F32)) * yn
            o_ref[rows, vs] = y.astype(o_ref.dtype)


def _ret_scan(lay, qk, vg, cos, sin, dl, gn_g, gn_b, qk_dim, v_dim, reverse, yf):
    q = LANES
    rows = 2 * q
    m = vg.shape[0]
    nsteps, row_block, pos_block = _scan_blocks(lay, rows, reverse)
    final = yf is not None
    hk, hv = qk_dim // RET_HEADS, v_dim // RET_HEADS
    tok = lambda w, cb: pl.BlockSpec((rows, w), lambda b, s: (row_block(b, s), cb))
    pos = pl.BlockSpec((rows, hk // 2), lambda b, s: (pos_block(b, s), 0))
    const = lambda shape: pl.BlockSpec(shape, lambda b, s: (0, 0))
    q_arr, k_arr, k_cb = (qk[0], qk[1], 0) if reverse else (qk, qk, 1)
    in_specs = [tok(qk_dim, 0), tok(qk_dim, k_cb), tok(v_dim, 0), pos, pos, const(dl.shape)]
    args = [q_arr, k_arr, vg, cos, sin, dl]
    if final:
        in_specs += [tok(v_dim, 0), tok(v_dim, 1), const((1, v_dim)), const((1, v_dim))]
        args += [yf, vg, gn_g.reshape(1, v_dim), gn_b.reshape(1, v_dim)]
    kern = functools.partial(_ret_scan_kernel, q=q, reverse=reverse, hk=hk, hv=hv, final=final)
    out_specs = [tok(v_dim, 0)]
    out_shape = [jax.ShapeDtypeStruct((m, v_dim), BF16)]
    if not reverse:
        out_specs += [tok(qk_dim, 0), tok(qk_dim, 0)]
        out_shape += [jax.ShapeDtypeStruct((m, qk_dim), BF16)] * 2
    return pl.pallas_call(
        kern,
        grid=(lay.batch, nsteps),
        in_specs=in_specs,
        out_specs=out_specs,
        out_shape=out_shape,
        scratch_shapes=[pltpu.VMEM((RET_HEADS, hk, hv), F32), pltpu.VMEM((RET_HEADS, q, q), F32),
                        pltpu.VMEM((RET_HEADS, q, LANES), F32), pltpu.VMEM((RET_HEADS, q, LANES), F32)],
        compiler_params=_cparams("parallel", "arbitrary"),
        name="ret_scan_bwd" if reverse else "ret_scan_fwd",
    )(*args)


def _vscan_kernel(*refs, kind, q, reverse, heads, hk, hv, final):
    it = iter(refs)
    if kind == "gla":
        q_ref, k_ref, v_ref, a_ref, wup_ref, ab_ref = (next(it) for _ in range(6))
    else:
        q_ref, v_ref, f_ref, lb_ref = (next(it) for _ in range(4))
    yf_ref = gate_ref = ng_ref = None
    if final:
        yf_ref, gate_ref, ng_ref = (next(it) for _ in range(3))
    o_ref = next(it)
    st_ref = next(it)

    @pl.when(pl.program_id(1) == 0)
    def _():
        st_ref[...] = jnp.zeros_like(st_ref)

    mask = _tri_mask(q, reverse)
    n_sub = o_ref.shape[0] // q
    for sub in (reversed(range(n_sub)) if reverse else range(n_sub)):
        rows = slice(sub * q, (sub + 1) * q)
        if kind == "gla":
            logit = _dot(a_ref[rows, :], wup_ref[...]) + ab_ref[...]
            la = _log_sigmoid(logit) * (1.0 / GLA_TAU)
            qf = q_ref[rows, :].astype(F32) * (hk ** -0.5)
            kf = k_ref[rows, :].astype(F32)
        else:
            lb = lb_ref[...]
            f = lb + (1.0 - lb) * _sigmoid(f_ref[rows, :].astype(F32))
            la = jnp.log(f)
            kf = 1.0 - f
            qf = _silu(q_ref[rows, :].astype(F32))

        cum = _cumsum_rows(la)
        total = cum[q - 1:q, :]
        if reverse:
            cum = total - cum + la
            ref = cum[q // 2 - 1:q // 2, :]
        else:
            ref = cum[q // 2:q // 2 + 1, :]
        e_rel = jnp.exp(cum - ref)
        q_rel = qf * e_rel
        k_rel = kf * (1.0 / e_rel)
        q_abs = (q_rel * jnp.exp(ref)).astype(BF16)
        k_out = (k_rel * jnp.exp(total - ref)).astype(BF16)
        q_rel = q_rel.astype(BF16)
        k_rel = k_rel.astype(BF16)
        e_tot = jnp.exp(total)

        for h in range(heads):
            ks = slice(h * hk, (h + 1) * hk)
            vs = slice(h * hv, (h + 1) * hv)
            v = v_ref[rows, vs]
            scores = jnp.where(mask, _dot_nt(q_rel[:, ks], k_rel[:, ks]), 0.0)
            st = st_ref[h]
            y = _dot(scores.astype(BF16), v) + _dot_nt(q_abs[:, ks], st.astype(BF16))
            st_ref[h] = st * e_tot[:, ks] + _dot_tn(v, k_out[:, ks])
            if final:
                y = y + yf_ref[rows, vs].astype(F32)
                ms = jnp.mean(y * y, axis=1, keepdims=True)
                y = y * lax.rsqrt(ms + EPS) * ng_ref[...] * _silu(gate_ref[rows, vs].astype(F32))
            o_ref[rows, vs] = y.astype(o_ref.dtype)


def _vscan(lay, kind, proj, extra, norm_g, heads, hk, hv, reverse, yf):
    q = 64
    rows = 4 * q
    m = proj.shape[0]
    nsteps, row_block, _ = _scan_blocks(lay, rows, reverse)
    final = yf is not None
    kd, vd = heads * hk, heads * hv
    tok = lambda w, cb: pl.BlockSpec((rows, w), lambda b, s: (row_block(b, s), cb))
    const = lambda shape, cb: pl.BlockSpec(shape, lambda b, s: (0, cb))
    d = 1 if reverse else 0
    if kind == "gla":
        a_low, wup, ab = extra
        in_specs = [tok(kd, 0), tok(kd, 1), tok(vd, 2 * kd // vd), tok(LANES, 0),
                    const((LANES, kd), d), const((1, kd), d)]
        args = [proj, proj, proj, a_low, wup, ab]
        gate_cb = 2 * kd // vd + 1
    else:
        (lb,) = extra
        in_specs = [tok(kd, 0), tok(vd, kd // vd), tok(kd, (kd + 2 * vd) // kd + d), const((1, kd), 0)]
        args = [proj, proj, proj, lb]
        gate_cb = kd // vd + 1
    if final:
        in_specs += [tok(vd, 0), tok(vd, gate_cb), const((1, hv), 0)]
        args += [yf, proj, norm_g.reshape(1, hv)]
    kern = functools.partial(_vscan_kernel, kind=kind, q=q, reverse=reverse, heads=heads, hk=hk, hv=hv,
                             final=final)
    return pl.pallas_call(
        kern,
        grid=(lay.batch, nsteps),
        in_specs=in_specs,
        out_specs=tok(vd, 0),
        out_shape=jax.ShapeDtypeStruct((m, vd), BF16),
        scratch_shapes=[pltpu.VMEM((heads, hv, hk), F32)],
        compiler_params=_cparams("parallel", "arbitrary"),
        name=f"{kind}_scan_{'bwd' if reverse else 'fwd'}",
    )(*args)


def _lower_bound_kernel(x_ref, o_ref):
    x = x_ref[...]
    mx = jnp.max(x, axis=0, keepdims=True)
    e = jnp.exp(x - mx)
    p = e / jnp.sum(e, axis=0, keepdims=True)
    acc = jnp.zeros_like(p[0:1])
    for i in range(x.shape[0]):
        acc = acc + p[i:i + 1]
        o_ref[i:i + 1, :] = acc - p[0:1]


def _lower_bounds(logits):
    return pl.pallas_call(_lower_bound_kernel, out_shape=jax.ShapeDtypeStruct(logits.shape, F32),
                          name="hgrn_lower_bounds")(logits)


def _conv_table(taps, bias, rows):
    c = bias.shape[0]
    tab = jnp.zeros((16, c), F32)
    tab = tab.at[jnp.asarray(rows)].set(taps).at[9].set(bias)
    return tab


def _ssd_mixer(lay, u, w_in, conv_w, conv_b, dt_bias, a_log, d_skip, d_model):
    d_inner = 2 * d_model
    heads = d_inner // SSD_HEAD_DIM
    conv_ch = d_inner + 2 * SSD_GROUPS * SSD_STATE
    cw = _conv_table(conv_w, conv_b, (3, 4, 5))
    cb = 1024
    blk = (d_model // 8, cb)
    w_xbc = _cast_pad(w_in[None], blk, blk, d_inner // cb, conv_ch // cb)[0]
    xbc = _proj_conv(lay, u, w_xbc, cw, tn=1024,
                     hr=2 * SUBLANES, vertical=False, lat_period=lay.seq, name="ssd_in_proj_conv")
    zproj = _matmul(lay, u, w_in, 0, d_inner)
    dtproj = _matmul(lay, u, w_in, d_inner + conv_ch, 2 * heads, out_dtype=F32)
    par = jnp.concatenate([dt_bias.reshape(1, 2 * heads), -jnp.exp(a_log.astype(F32)).reshape(1, 2 * heads),
                           jnp.zeros((6, 2 * heads), F32)], axis=0)
    dskip = jnp.repeat(d_skip, SSD_HEAD_DIM).reshape(1, d_inner)
    yf = _ssd_scan(lay, xbc, zproj, dtproj, par, dskip, d_inner, False, None)
    return _ssd_scan(lay, xbc, zproj, dtproj, par, dskip, d_inner, True, yf)


def _deinterleave_kernel(w_ref, o_ref):
    n = w_ref.shape[1]
    src = lax.broadcasted_iota(jnp.int32, (n, n), 0)
    dst = lax.broadcasted_iota(jnp.int32, (n, n), 1)
    want = jnp.where(dst < n // 2, 2 * dst, 2 * (dst - n // 2) + 1)
    perm = jnp.where(src == want, 1.0, 0.0).astype(BF16)
    o_ref[...] = _dot(w_ref[...].astype(BF16), perm).astype(BF16)


def _deinterleave_heads(w, heads, hk):
    d = w.shape[0]
    return pl.pallas_call(
        _deinterleave_kernel,
        grid=(heads,),
        in_specs=[pl.BlockSpec((d, hk), lambda h: (0, h))],
        out_specs=pl.BlockSpec((d, hk), lambda h: (0, h)),
        out_shape=jax.ShapeDtypeStruct((d, heads * hk), BF16),
        compiler_params=_cparams("parallel"),
        name="ret_deinterleave",
    )(w)


def _ret_mixer(lay, u, w_in, decay_logit, gn_g, gn_b, d_model):
    qk, vd = d_model, 2 * d_model
    hk = qk // RET_HEADS
    w_qk = _deinterleave_heads(w_in, 2 * RET_HEADS, hk)
    qkp = _matmul(lay, u, w_qk, 0, 2 * qk)
    vgp = _matmul(lay, u, w_in, 2 * qk, 2 * vd)
    half = hk // 2
    inv_freq = 1.0 / (10000.0 ** jnp.linspace(0.0, 1.0, half, dtype=F32))
    ang = jnp.arange(lay.ctx + lay.seq, dtype=F32)[:, None] * inv_freq[None, :]
    cos, sin = jnp.cos(ang), jnp.sin(ang)
    dl = jnp.repeat(decay_logit.astype(F32), LANES, axis=1)
    dl = jnp.concatenate([dl, jnp.zeros((6, dl.shape[1]), F32)], axis=0)
    yf, q_rot, k_rot = _ret_scan(lay, qkp, vgp, cos, sin, dl, gn_g, gn_b, qk, vd, False, None)
    return _ret_scan(lay, (q_rot, k_rot), vgp, cos, sin, dl, gn_g, gn_b, qk, vd, True, yf)[0]


def _gla_mixer(lay, u, w_in, w_alpha_up, alpha_b, norm_g, d_model):
    kd, vd = d_model // 2, d_model
    n_main = 2 * kd + 2 * vd
    proj = _matmul(lay, u, w_in, 0, n_main)
    w_low = jnp.pad(w_in[:, n_main:], ((0, 0), (0, LANES - 2 * GLA_RANK))).astype(BF16)
    a_low = _matmul(lay, u, w_low, 0, LANES)
    wup = jnp.zeros((LANES, 2 * kd), F32)
    wup = wup.at[0:GLA_RANK, 0:kd].set(w_alpha_up[0]).at[GLA_RANK:2 * GLA_RANK, kd:].set(w_alpha_up[1])
    ab = alpha_b.reshape(1, 2 * kd)
    args = (lay, "gla", proj, (a_low, wup.astype(BF16), ab), norm_g, GLA_HEADS, kd // GLA_HEADS,
            vd // GLA_HEADS)
    yf = _vscan(*args, False, None)
    return _vscan(*args, True, yf)


def _hgrn_mixer(lay, u, w_in, lower_bound, norm_g, d_model):
    heads = d_model // HGRN_HEAD_K
    proj = _matmul(lay, u, w_in, 0, w_in.shape[1])
    args = (lay, "hgrn", proj, (lower_bound.reshape(1, d_model),), norm_g, heads, HGRN_HEAD_K,
            d_model // heads)
    yf = _vscan(*args, False, None)
    return _vscan(*args, True, yf)


def kernel(x, c, ctx, c_ctx, mod_w, mod_b, ln_mix_g, ln_mix_b, ln_ffn_g, ln_ffn_b, ffn_w_up, ffn_conv_w, ffn_conv_b, ffn_w_down, hgrn_lb_logits, ssd_w_in, ssd_conv_w, ssd_conv_b, ssd_dt_bias, ssd_a_log, ssd_d, ssd_norm_g, ssd_w_out, ret_w_in, ret_decay_logit, ret_gn_g, ret_gn_b, ret_w_out, gla_w_in, gla_w_alpha_up, gla_alpha_b, gla_norm_g, gla_w_out, hgrn_w_in, hgrn_norm_g, hgrn_w_out):
    batch, seq, d = x.shape
    ctx_len = ctx.shape[1]
    depth = mod_w.shape[0]
    hidden = ffn_w_down.shape[1]
    lay = _Layout(batch, seq, ctx_len)
    alpha = (2.0 * depth) ** 0.25
    n_mixers = 4
    assert GRID_W & (GRID_W - 1) == 0 and ctx_len & (ctx_len - 1) == 0 and seq & (seq - 1) == 0

    c_rows = jnp.concatenate([c, c_ctx[None], jnp.zeros((SUBLANES - batch - 1, d), F32)], axis=0)
    mods = _modulation(c_rows, mod_w, mod_b).reshape(depth, SUBLANES, 1, 6 * d)
    SH_M, SC_M, G_M, SH_F, SC_F, G_F = range(6)

    lower_bounds = _lower_bounds(hgrn_lb_logits.astype(F32))

    h, u = _modulate(lay, x.reshape(lay.m_lat, d), ctx.reshape(lay.m_ctx, d), mods, 0, SC_M, SH_M)

    ffn_tn = 512
    hp = -(-hidden // ffn_tn) * ffn_tn
    tk_ffn = hp // 4 if (hp // 4) % LANES == 0 else ffn_tn
    pad = hp - hidden
    del pad
    w_up_all = _cast_pad(ffn_w_up, (d // 8, hidden), (d // 8, hp))
    w_down_all = _cast_pad(ffn_w_down, (hidden, d // 8), (hp, d // 8))
    cw_all = jnp.zeros((depth, 16, hp), F32)
    cw_all = cw_all.at[:, 0:9, :hidden].set(ffn_conv_w.reshape(depth, 9, hidden))
    cw_all = cw_all.at[:, 9, :hidden].set(ffn_conv_b)
    for i in range(depth):
        kind, j = i % n_mixers, i // n_mixers
        last = i == depth - 1
        m_rows = lay.m_lat if last else None
        rms_gain = None
        if kind == 0:
            y = _ssd_mixer(lay, u, ssd_w_in[j], ssd_conv_w[j], ssd_conv_b[j], ssd_dt_bias[j], ssd_a_log[j],
                           ssd_d[j], d)
            w_out, rms_gain = ssd_w_out[j], ssd_norm_g[j]
        elif kind == 1:
            y = _ret_mixer(lay, u, ret_w_in[j], ret_decay_logit[j], ret_gn_g[j], ret_gn_b[j], d)
            w_out = ret_w_out[j]
        elif kind == 2:
            y = _gla_mixer(lay, u, gla_w_in[j], gla_w_alpha_up[j], gla_alpha_b[j], gla_norm_g[j], d)
            w_out = gla_w_out[j]
        else:
            y = _hgrn_mixer(lay, u, hgrn_w_in[j], lower_bounds[i], hgrn_norm_g[j], d)
            w_out = hgrn_w_out[j]
        kd = w_out.shape[0]
        h, u = _matmul_ln(lay, y, w_out.astype(BF16), h, mods, i, G_M, ln_mix_g[i], ln_mix_b[i], alpha,
                          next_mod=(i, SC_F, SH_F), rms_gain=rms_gain, tk=min(kd, 2048), m_rows=m_rows)

        act = _proj_conv(lay, u, w_up_all, cw_all, layer=i, tn=ffn_tn, hr=GRID_W, vertical=True,
                         lat_period=GRID_W, m_rows=m_rows, name="ffn_up_conv")
        nxt = (i + 1, SC_M, SH_M) if not last else None
        h, u = _matmul_ln(lay, act, w_down_all, h, mods, i, G_F, ln_ffn_g[i], ln_ffn_b[i], alpha,
                          next_mod=nxt, tk=tk_ffn, m_rows=m_rows, w_layer=i)
    return h.reshape(batch, seq, d)
```
